```python
import jax, jax.numpy as jnp
from jax import lax
import numpy as np

D_MODEL = 2048
BATCH = 1
SEQ = 8192
DEPTH = 4

MLSTM_WIDTH = D_MODEL
MLSTM_HEADS = 4
MLSTM_HEAD_DIM = MLSTM_WIDTH // MLSTM_HEADS
CHUNK = 128
QK_CONV = 5
FGATE_BIAS_LO = 3.0
FGATE_BIAS_HI = 6.0
CONV_WIDTH = D_MODEL
CONV_KERNEL = 31
EPS = 1e-6

_SIZES = (MLSTM_WIDTH,) * 5 + (4 * MLSTM_HEADS,) + (CONV_WIDTH,) * 3 + (D_MODEL,) * 2
N_IN = int(sum(_SIZES))
_OFFSETS = tuple(int(o) for o in np.cumsum(_SIZES)[:-1])
GATE_OFFSET = 5 * MLSTM_WIDTH

kernel_name = "bidir_mlstm_conformer_conv_gated_hybrid"


def rms_norm(x, w):
    xf = x.astype(jnp.float32)
    y = xf * lax.rsqrt(jnp.mean(xf * xf, axis=-1, keepdims=True) + EPS)
    return (y * w.astype(jnp.float32)).astype(x.dtype)


def layer_norm(x, w, b):
    xf = x.astype(jnp.float32)
    mu = jnp.mean(xf, axis=-1, keepdims=True)
    var = jnp.mean(jnp.square(xf - mu), axis=-1, keepdims=True)
    y = (xf - mu) * lax.rsqrt(var + EPS)
    return (y * w.astype(jnp.float32) + b.astype(jnp.float32)).astype(x.dtype)


def depthwise_conv(x, w):
    K = w.shape[0]
    return lax.conv_general_dilated(
        x, w[:, None, :].astype(x.dtype), window_strides=(1,),
        padding=[(K // 2, K // 2)],
        dimension_numbers=("NWC", "WIO", "NWC"),
        feature_group_count=x.shape[-1])


def mlstm_chunkwise(q, k, v, log_i, log_f):
    N, H, S, dh = q.shape
    nc = S // CHUNK

    def to_chunks(t):
        t = t.reshape((N, H, nc, CHUNK) + t.shape[3:])
        return jnp.moveaxis(t, 2, 0)

    xs = tuple(to_chunks(t) for t in (q, k, v, log_i, log_f))
    causal_in_chunk = jnp.tril(jnp.ones((CHUNK, CHUNK), dtype=bool))

    def step(carry, inp):
        C, n, m = carry
        qj, kj, vj, ij, fj = inp
        g = jnp.cumsum(fj, axis=-1)
        a = g + m[..., None]
        D = g[..., :, None] - g[..., None, :] + ij[..., None, :]
        D = jnp.where(causal_in_chunk, D, -jnp.inf)
        m_row = jnp.maximum(a, jnp.max(D, axis=-1))
        w_intra = jnp.exp(D - m_row[..., None])
        w_inter = jnp.exp(a - m_row)
        s = jnp.einsum("nhld,nhsd->nhls", qj, kj) * w_intra
        num = (jnp.einsum("nhls,nhse->nhle", s, vj)
               + w_inter[..., None] * jnp.einsum("nhld,nhde->nhle", qj, C))
        den = jnp.sum(s, axis=-1) + w_inter * jnp.einsum("nhld,nhd->nhl", qj, n)
        h = num / jnp.maximum(jnp.abs(den), jnp.exp(-m_row))[..., None]
        G = g[..., -1]
        b = G[..., None] - g + ij
        m_new = jnp.maximum(G + m, jnp.max(b, axis=-1))
        wk = jnp.exp(b - m_new[..., None])[..., None] * kj
        decay = jnp.exp(G + m - m_new)
        C_new = decay[..., None, None] * C + jnp.einsum("nhsd,nhse->nhde", wk, vj)
        n_new = decay[..., None] * n + jnp.sum(wk, axis=2)
        return (C_new, n_new, m_new), h

    init = (jnp.zeros((N, H, dh, dh), jnp.float32),
            jnp.zeros((N, H, dh), jnp.float32),
            jnp.zeros((N, H), jnp.float32))
    _, hs = lax.scan(step, init, xs)
    return jnp.moveaxis(hs, 0, 2).reshape(N, H, S, dh)


def hybrid_layer(x, norm_w, w_in, b_in, qk_conv_w, mlstm_norm_w, w_a,
                 dw_w, dw_b, ln_w, ln_b, w_b, w_out):
    B, S, _ = x.shape
    H, dh = MLSTM_HEADS, MLSTM_HEAD_DIM
    f32 = jnp.float32
    hn = rms_norm(x, norm_w)
    u = jnp.einsum("bsd,dn->bsn", hn, w_in) + b_in
    q, k, v, o, z_a, gates, glu_a, glu_b, z_b, g_a, g_b = jnp.split(u, _OFFSETS, axis=-1)

    qk = jax.nn.silu(depthwise_conv(jnp.concatenate([q, k], axis=-1), qk_conv_w))
    q, k = jnp.split(qk, 2, axis=-1)

    def heads(t):
        return t.reshape(B, S, H, dh).transpose(0, 2, 1, 3).astype(f32)

    qh, kh, vh = heads(q), heads(k) * (dh ** -0.5), heads(v)
    gates = gates.astype(f32).reshape(B, S, 4, H).transpose(2, 0, 3, 1)
    i_fw, f_fw, i_bw, f_bw = gates[0], gates[1], gates[2], gates[3]
    flip = lambda t: jnp.flip(t, axis=2)
    both = lambda fw, bw: jnp.concatenate([fw, flip(bw)], axis=0)
    h2 = mlstm_chunkwise(both(qh, qh), both(kh, kh), both(vh, vh),
                         both(i_fw, i_bw), jax.nn.log_sigmoid(both(f_fw, f_bw)))
    ht = h2[:B] + flip(h2[B:])
    ht = jax.nn.sigmoid(heads(o)) * ht
    ht = ht * lax.rsqrt(jnp.mean(ht * ht, axis=-1, keepdims=True) + EPS)
    ya = ht.transpose(0, 2, 1, 3).reshape(B, S, MLSTM_WIDTH) * mlstm_norm_w.astype(f32)
    ya = ya.astype(x.dtype) * jax.nn.silu(z_a)
    y_a = jnp.einsum("bsw,wd->bsd", ya, w_a)

    c = glu_a * jax.nn.sigmoid(glu_b)
    c = depthwise_conv(c, dw_w) + dw_b
    c = jax.nn.silu(layer_norm(c, ln_w, ln_b)) * jax.nn.silu(z_b)
    y_b = jnp.einsum("bsw,wd->bsd", c, w_b)

    merged = jax.nn.sigmoid(g_a) * y_a + jax.nn.sigmoid(g_b) * y_b
    return x + jnp.einsum("bsd,de->bse", merged, w_out)


def setup_inputs(seed: int = 0) -> dict:
    key = jax.random.key(seed)
    ks = jax.random.split(key, 16)
    f32 = jnp.float32
    L, D, H = DEPTH, D_MODEL, MLSTM_HEADS
    nrm = lambda k, shape, scale: scale * jax.random.normal(k, shape, f32)
    x = nrm(ks[0], (BATCH, SEQ, D), 1.0)
    norm_w = 1.0 + nrm(ks[1], (L, D), 0.02)
    w_in = nrm(ks[2], (L, D, N_IN), D ** -0.5)
    b_in = nrm(ks[3], (L, N_IN), 0.02)
    f_bias = jnp.linspace(FGATE_BIAS_LO, FGATE_BIAS_HI, H, dtype=f32)
    b_in = b_in.at[:, GATE_OFFSET + H:GATE_OFFSET + 2 * H].add(f_bias)
    b_in = b_in.at[:, GATE_OFFSET + 3 * H:GATE_OFFSET + 4 * H].add(f_bias)
    qk_conv_w = nrm(ks[4], (L, QK_CONV, 2 * MLSTM_WIDTH), QK_CONV ** -0.5)
    mlstm_norm_w = 1.0 + nrm(ks[5], (L, MLSTM_WIDTH), 0.02)
    w_a = nrm(ks[6], (L, MLSTM_WIDTH, D), MLSTM_WIDTH ** -0.5)
    dw_w = nrm(ks[7], (L, CONV_KERNEL, CONV_WIDTH), CONV_KERNEL ** -0.5)
    dw_b = nrm(ks[8], (L, CONV_WIDTH), 0.02)
    ln_w = 1.0 + nrm(ks[9], (L, CONV_WIDTH), 0.02)
    ln_b = nrm(ks[10], (L, CONV_WIDTH), 0.02)
    w_b = nrm(ks[11], (L, CONV_WIDTH, D), CONV_WIDTH ** -0.5)
    w_out = nrm(ks[12], (L, D, D), D ** -0.5)
    final_norm_w = 1.0 + nrm(ks[13], (D,), 0.02)
    return {"x": x, "norm_w": norm_w, "w_in": w_in, "b_in": b_in,
            "qk_conv_w": qk_conv_w, "mlstm_norm_w": mlstm_norm_w, "w_a": w_a,
            "dw_w": dw_w, "dw_b": dw_b, "ln_w": ln_w, "ln_b": ln_b, "w_b": w_b,
            "w_out": w_out, "final_norm_w": final_norm_w}


def reference(x, norm_w, w_in, b_in, qk_conv_w, mlstm_norm_w, w_a, dw_w, dw_b,
              ln_w, ln_b, w_b, w_out, final_norm_w):
    for l in range(DEPTH):
        x = hybrid_layer(x, norm_w[l], w_in[l], b_in[l], qk_conv_w[l], mlstm_norm_w[l],
                         w_a[l], dw_w[l], dw_b[l], ln_w[l], ln_b[l], w_b[l], w_out[l])
    return rms_norm(x, final_norm_w)
```

```python
import functools

import jax
import jax.numpy as jnp
from jax import lax
from jax.experimental import pallas as pl
from jax.experimental.pallas import tpu as pltpu

F32 = jnp.float32
BF16 = jnp.bfloat16

EPS = 1e-6
CHUNK = 128
LANES = 128
SUBLANES = 8
VMEM_LIMIT_BYTES = 56 * 2**20
CONV_PAD = 16
CONV_ROWS = 128


def _params(*sem):
    return pltpu.CompilerParams(dimension_semantics=sem, vmem_limit_bytes=VMEM_LIMIT_BYTES)


def _sigmoid(x):
    return 1.0 / (1.0 + jnp.exp(-x))


def _silu(x):
    return x * _sigmoid(x)


def _inproj_kernel(x_ref, nw_ref, w_ref, b_ref, wg_ref, bg_ref, u_ref, g_ref, hn_ref, *, row_step):
    @pl.when(pl.program_id(1) == 0)
    def _():
        for r in range(0, x_ref.shape[0], row_step):
            x = x_ref[r:r + row_step, :]
            ms = jnp.mean(x * x, axis=-1, keepdims=True)
            hn_ref[r:r + row_step, :] = ((x * lax.rsqrt(ms + EPS)) * nw_ref[...]).astype(BF16)
        g_ref[...] = jnp.dot(hn_ref[...], wg_ref[...], preferred_element_type=F32) + bg_ref[...]

    u_ref[...] = jnp.dot(hn_ref[...], w_ref[...], preferred_element_type=F32) + b_ref[...]


def _inproj(x, nw, w, b, wg, bg, *, tm, tn):
    S, D = x.shape
    N = w.shape[1]
    NG = wg.shape[1]
    return pl.pallas_call(
        functools.partial(_inproj_kernel, row_step=min(tm, 128)),
        grid=(S // tm, N // tn),
        in_specs=[pl.BlockSpec((tm, D), lambda i, j: (i, 0)),
                  pl.BlockSpec((1, D), lambda i, j: (0, 0)),
                  pl.BlockSpec((D, tn), lambda i, j: (0, j)),
                  pl.BlockSpec((1, tn), lambda i, j: (0, j)),
                  pl.BlockSpec((D, NG), lambda i, j: (0, 0)),
                  pl.BlockSpec((1, NG), lambda i, j: (0, 0))],
        out_specs=[pl.BlockSpec((tm, tn), lambda i, j: (i, j)),
                   pl.BlockSpec((tm, NG), lambda i, j: (i, 0))],
        out_shape=[jax.ShapeDtypeStruct((S, N), F32), jax.ShapeDtypeStruct((S, NG), F32)],
        scratch_shapes=[pltpu.VMEM((tm, D), BF16)],
        compiler_params=_params("arbitrary", "arbitrary"),
        name="inproj",
    )(x, nw, w, b, wg, bg)


def _stripe_conv_kernel(*refs, n_in, ktaps, pre, post, transpose_out, scale):
    in_refs = refs[:n_in]
    w_ref, b_ref, o_ref, pad_ref = refs[n_in:n_in + 4]
    S, cw = in_refs[0].shape
    P, R = CONV_PAD, CONV_ROWS
    win_rows = R + 2 * P
    half = ktaps // 2

    pad_ref[0:P, :] = jnp.zeros((P, cw), F32)
    pad_ref[P + S:P + S + P, :] = jnp.zeros((P, cw), F32)

    def fill(i, c):
        r0 = pl.multiple_of(i * R, R)
        vals = [r[pl.ds(r0, R), :] for r in in_refs]
        pad_ref[pl.ds(P + r0, R), :] = pre(*vals)
        return c

    lax.fori_loop(0, S // R, fill, 0)

    def body(i, c):
        r0 = pl.multiple_of(i * R, R)
        win = pad_ref[pl.ds(r0, win_rows), :]
        acc = jnp.zeros((R, cw), F32) + b_ref[...]
        for sub in range(SUBLANES):
            taps = [d for d in range(ktaps) if (P + d - half) % SUBLANES == sub]
            if not taps:
                continue
            wsub = win if sub == 0 else pltpu.roll(win, win_rows - sub, 0)
            for d in taps:
                a = (P + d - half) // SUBLANES
                acc = acc + w_ref[d:d + 1, :] * wsub[SUBLANES * a:SUBLANES * a + R, :]
        y = post(acc)
        if scale != 1.0:
            y = y * scale
        if transpose_out:
            o_ref[:, pl.ds(r0, R)] = y.T.astype(o_ref.dtype)
        else:
            o_ref[pl.ds(r0, R), :] = y.astype(o_ref.dtype)
        return c

    lax.fori_loop(0, S // R, body, 0)


def _stripe_conv(inputs, col_blocks, w, b, *, ncols, cw, pre, post, transpose_out, scale, out_dtype, name):
    S = inputs[0].shape[0]
    ktaps = w.shape[0]
    nj = ncols // cw
    in_specs = [pl.BlockSpec((S, cw), functools.partial(lambda j, off: (0, off + j), off=cb * nj))
                for cb in col_blocks]
    in_specs += [pl.BlockSpec((ktaps, cw), lambda j: (0, j)), pl.BlockSpec((1, cw), lambda j: (0, j))]
    if transpose_out:
        out_spec = pl.BlockSpec((cw, S), lambda j: (j, 0))
        out_shape = jax.ShapeDtypeStruct((ncols, S), out_dtype)
    else:
        out_spec = pl.BlockSpec((S, cw), lambda j: (0, j))
        out_shape = jax.ShapeDtypeStruct((S, ncols), out_dtype)
    return pl.pallas_call(
        functools.partial(_stripe_conv_kernel, n_in=len(inputs), ktaps=ktaps, pre=pre, post=post,
                          transpose_out=transpose_out, scale=scale),
        grid=(nj,),
        in_specs=in_specs,
        out_specs=out_spec,
        out_shape=out_shape,
        scratch_shapes=[pltpu.VMEM((S + 2 * CONV_PAD, cw), F32)],
        compiler_params=_params("arbitrary"),
        name=name,
    )(*inputs, w, b)


def _split3(x):
    a = x.astype(BF16)
    r = x - a.astype(F32)
    b = r.astype(BF16)
    c = (r - b.astype(F32)).astype(BF16)
    return a, b, c


def _gateprep_kernel(gf_ref, gb_ref, *out_and_scratch, heads):
    outs = out_and_scratch[:10]
    m_ref = out_and_scratch[10]
    L = CHUNK

    @pl.when(pl.program_id(0) == 0)
    def _():
        m_ref[...] = jnp.zeros(m_ref.shape, F32)

    row = lax.broadcasted_iota(jnp.int32, (L, L), 0)
    col = lax.broadcasted_iota(jnp.int32, (L, L), 1)
    for d, g_ref in enumerate((gf_ref, gb_ref)):
        wi_ref, wr_ref, er_ref, wk_ref, dc_ref = outs[5 * d:5 * d + 5]
        keep = (col <= row) if d == 0 else (col >= row)
        tri = keep.astype(BF16)
        ti = g_ref[:, 0:LANES]
        tf = g_ref[:, LANES:2 * LANES]
        ls = jnp.minimum(tf, 0.0) - jnp.log1p(jnp.exp(-jnp.abs(tf)))
        g = sum(jnp.dot(tri, p, preferred_element_type=F32) for p in _split3(ls))
        gT = g.T
        iT = ti.T
        for h in range(heads):
            c = heads * d + h
            m = m_ref[c:c + 1, :]
            g_col = g[:, c:c + 1]
            g_row = gT[c:c + 1, :]
            i_row = iT[c:c + 1, :]
            a = g_col + m
            dmat = jnp.where(keep, g_col - g_row + i_row, -jnp.inf)
            m_rows = jnp.maximum(a, jnp.max(dmat, axis=-1, keepdims=True))
            wi_ref[h] = jnp.exp(dmat - m_rows)
            wr_ref[h] = jnp.exp(a - m_rows)
            er_ref[h] = jnp.exp(-m_rows)
            gtot = g_row[:, L - 1:L] if d == 0 else g_row[:, 0:1]
            b_row = gtot - g_row + i_row
            m_new = jnp.maximum(gtot + m, jnp.max(b_row, axis=-1, keepdims=True))
            wk_ref[h] = jnp.broadcast_to(jnp.exp(b_row - m_new), (SUBLANES, L))
            dec = jnp.exp(gtot + m - m_new)
            dc_ref[h] = jnp.broadcast_to(dec, (SUBLANES, L))
            m_ref[c:c + 1, :] = m_new


def _gateprep(gates, heads):
    S = gates.shape[0]
    nc = S // CHUNK
    L = CHUNK
    big = lambda idx: pl.BlockSpec((heads, L, L), idx)
    small = lambda idx: pl.BlockSpec((heads, SUBLANES, L), idx)
    fw = lambda j: (0, j, 0)
    bw = lambda j: (0, nc - 1 - j, 0)
    out_specs, out_shape = [], []
    for idx in (fw, bw):
        out_specs += [big(idx), big(idx), big(idx), small(idx), small(idx)]
        out_shape += [jax.ShapeDtypeStruct((heads, S, L), F32)] * 3
        out_shape += [jax.ShapeDtypeStruct((heads, nc * SUBLANES, L), F32)] * 2
    return pl.pallas_call(
        functools.partial(_gateprep_kernel, heads=heads),
        grid=(nc,),
        in_specs=[pl.BlockSpec((L, 2 * LANES), lambda j: (j, 0)),
                  pl.BlockSpec((L, 2 * LANES), lambda j: (nc - 1 - j, 0))],
        out_specs=out_specs,
        out_shape=out_shape,
        scratch_shapes=[pltpu.VMEM((2 * heads, L), F32)],
        compiler_params=_params("arbitrary"),
        name="gateprep",
    )(gates, gates)


def _mlstm_kernel(*refs):
    ins = refs[:16]
    hf_ref, hb_ref, c_ref, n_ref = refs[16:20]
    L = CHUNK

    @pl.when(pl.program_id(1) == 0)
    def _():
        c_ref[...] = jnp.zeros(c_ref.shape, F32)
        n_ref[...] = jnp.zeros(n_ref.shape, F32)

    for d, h_ref in enumerate((hf_ref, hb_ref)):
        q_ref, kt_ref, v_ref, wi_ref, wr_ref, er_ref, wk_ref, dc_ref = ins[8 * d:8 * d + 8]
        dh = q_ref.shape[1]
        q = q_ref[...]
        kt = kt_ref[...]
        v = v_ref[...].astype(BF16)
        cmat = c_ref[d]
        nvec = n_ref[d]
        rhs = jnp.concatenate([kt.astype(BF16), nvec.astype(BF16)], axis=1)
        qk2 = jnp.dot(q, rhs, preferred_element_type=F32)
        s = qk2[:, :L] * wi_ref[...]
        qn = qk2[:, L:]
        w_inter = wr_ref[...]
        den = jnp.sum(s, axis=-1, keepdims=True) + w_inter * qn
        rden = 1.0 / jnp.maximum(jnp.abs(den), er_ref[...])
        sv = jnp.dot(s.astype(BF16), v, preferred_element_type=F32)
        qc = jnp.dot(q, cmat.astype(BF16), preferred_element_type=F32)
        for blk in range(dh // L):
            sl = slice(blk * L, (blk + 1) * L)
            h_ref[:, sl] = (sv[:, sl] + w_inter * qc[:, sl]) * rden
        wkt = kt * wk_ref[0:1, :]
        dec = dc_ref[0:1, :]
        n_ref[d] = dec * nvec + jnp.sum(wkt, axis=-1, keepdims=True)
        dcm = jnp.dot(wkt.astype(BF16), v, preferred_element_type=F32)
        dec_w = jnp.concatenate([dec] * (dh // L), axis=1)
        c_ref[d] = dec_w * cmat + dcm


def _mlstm(qc, kt, u, prep, *, heads, v_block):
    S, W = qc.shape
    dh = W // heads
    nc = S // CHUNK
    L = CHUNK
    in_specs, args = [], []
    for d in range(2):
        cidx = (lambda j: j) if d == 0 else (lambda j: nc - 1 - j)
        wi, wr, er, wk, dc = prep[5 * d:5 * d + 5]
        in_specs += [
            pl.BlockSpec((L, dh), functools.partial(lambda h, j, f: (f(j), h), f=cidx)),
            pl.BlockSpec((dh, L), functools.partial(lambda h, j, f: (h, f(j)), f=cidx)),
            pl.BlockSpec((L, dh), functools.partial(lambda h, j, f: (f(j), v_block * heads + h), f=cidx)),
            pl.BlockSpec((None, L, L), functools.partial(lambda h, j, f: (h, f(j), 0), f=cidx)),
            pl.BlockSpec((None, L, L), functools.partial(lambda h, j, f: (h, f(j), 0), f=cidx)),
            pl.BlockSpec((None, L, L), functools.partial(lambda h, j, f: (h, f(j), 0), f=cidx)),
            pl.BlockSpec((None, SUBLANES, L), functools.partial(lambda h, j, f: (h, f(j), 0), f=cidx)),
            pl.BlockSpec((None, SUBLANES, L), functools.partial(lambda h, j, f: (h, f(j), 0), f=cidx)),
        ]
        args += [qc, kt, u, wi, wr, er, wk, dc]
    out_specs = [pl.BlockSpec((L, dh), lambda h, j: (j, h)),
                 pl.BlockSpec((L, dh), lambda h, j: (nc - 1 - j, h))]
    return pl.pallas_call(
        _mlstm_kernel,
        grid=(heads, nc),
        in_specs=in_specs,
        out_specs=out_specs,
        out_shape=[jax.ShapeDtypeStruct((S, W), F32)] * 2,
        scratch_shapes=[pltpu.VMEM((2, dh, dh), F32), pltpu.VMEM((2, dh, L), F32)],
        compiler_params=_params("arbitrary", "arbitrary"),
        name="mlstm",
    )(*args)


def _post_a_kernel(hf_ref, hb_ref, o_ref, z_ref, nw_ref, y_ref, *, heads):
    W = hf_ref.shape[1]
    dh = W // heads
    for h in range(heads):
        sl = slice(h * dh, (h + 1) * dh)
        ht = _sigmoid(o_ref[:, sl]) * (hf_ref[:, sl] + hb_ref[:, sl])
        ht = ht * lax.rsqrt(jnp.mean(ht * ht, axis=-1, keepdims=True) + EPS)
        y_ref[:, sl] = ((ht * nw_ref[:, sl]) * _silu(z_ref[:, sl])).astype(y_ref.dtype)


def _post_a(hf, hb, u, nw, *, heads, o_block, z_block, tm):
    S, W = hf.shape
    row = lambda i: (i, 0)
    return pl.pallas_call(
        functools.partial(_post_a_kernel, heads=heads),
        grid=(S // tm,),
        in_specs=[pl.BlockSpec((tm, W), row), pl.BlockSpec((tm, W), row),
                  pl.BlockSpec((tm, W), lambda i: (i, o_block)),
                  pl.BlockSpec((tm, W), lambda i: (i, z_block)),
                  pl.BlockSpec((1, W), lambda i: (0, 0))],
        out_specs=pl.BlockSpec((tm, W), row),
        out_shape=jax.ShapeDtypeStruct((S, W), BF16),
        compiler_params=_params("arbitrary"),
        name="post_a",
    )(hf, hb, u, u, nw)


def _post_b_kernel(c_ref, z_ref, lw_ref, lb_ref, y_ref):
    c = c_ref[...]
    mu = jnp.mean(c, axis=-1, keepdims=True)
    cc = c - mu
    var = jnp.mean(cc * cc, axis=-1, keepdims=True)
    y = cc * lax.rsqrt(var + EPS) * lw_ref[...] + lb_ref[...]
    y_ref[...] = (_silu(y) * _silu(z_ref[...])).astype(y_ref.dtype)


def _post_b(c, u, lw, lb, *, z_block, tm):
    S, W = c.shape
    return pl.pallas_call(
        _post_b_kernel,
        grid=(S // tm,),
        in_specs=[pl.BlockSpec((tm, W), lambda i: (i, 0)),
                  pl.BlockSpec((tm, W), lambda i: (i, z_block)),
                  pl.BlockSpec((1, W), lambda i: (0, 0)),
                  pl.BlockSpec((1, W), lambda i: (0, 0))],
        out_specs=pl.BlockSpec((tm, W), lambda i: (i, 0)),
        out_shape=jax.ShapeDtypeStruct((S, W), BF16),
        compiler_params=_params("arbitrary"),
        name="post_b",
    )(c, u, lw, lb)


def _merge_kernel(ya_ref, yb_ref, wa_ref, wb_ref, ga_ref, gb_ref, m_ref):
    pa = jnp.dot(ya_ref[...], wa_ref[...], preferred_element_type=F32)
    pb = jnp.dot(yb_ref[...], wb_ref[...], preferred_element_type=F32)
    m_ref[...] = (_sigmoid(ga_ref[...]) * pa + _sigmoid(gb_ref[...]) * pb).astype(m_ref.dtype)


def _merge(ya, yb, wa, wb, u, *, ga_block, gb_block, tm, tn):
    S, W = ya.shape
    D = wa.shape[1]
    nj = D // tn
    return pl.pallas_call(
        _merge_kernel,
        grid=(S // tm, nj),
        in_specs=[pl.BlockSpec((tm, W), lambda i, j: (i, 0)),
                  pl.BlockSpec((tm, W), lambda i, j: (i, 0)),
                  pl.BlockSpec((W, tn), lambda i, j: (0, j)),
                  pl.BlockSpec((W, tn), lambda i, j: (0, j)),
                  pl.BlockSpec((tm, tn), lambda i, j: (i, ga_block * nj + j)),
                  pl.BlockSpec((tm, tn), lambda i, j: (i, gb_block * nj + j))],
        out_specs=pl.BlockSpec((tm, tn), lambda i, j: (i, j)),
        out_shape=jax.ShapeDtypeStruct((S, D), BF16),
        compiler_params=_params("arbitrary", "arbitrary"),
        name="merge",
    )(ya, yb, wa, wb, u, u)


def _outproj_kernel(m_ref, w_ref, x_ref, o_ref):
    o_ref[...] = x_ref[...] + jnp.dot(m_ref[...], w_ref[...], preferred_element_type=F32)


def _outproj(m, w, x, *, tm, tn):
    S, D = x.shape
    return pl.pallas_call(
        _outproj_kernel,
        grid=(S // tm, D // tn),
        in_specs=[pl.BlockSpec((tm, D), lambda i, j: (i, 0)),
                  pl.BlockSpec((D, tn), lambda i, j: (0, j)),
                  pl.BlockSpec((tm, tn), lambda i, j: (i, j))],
        out_specs=pl.BlockSpec((tm, tn), lambda i, j: (i, j)),
        out_shape=jax.ShapeDtypeStruct((S, D), F32),
        compiler_params=_params("arbitrary", "arbitrary"),
        name="outproj",
    )(m, w, x)


def _final_norm_kernel(x_ref, w_ref, o_ref):
    x = x_ref[...]
    o_ref[...] = (x * lax.rsqrt(jnp.mean(x * x, axis=-1, keepdims=True) + EPS)) * w_ref[...]


def _final_norm(x, w, *, tm):
    S, D = x.shape
    return pl.pallas_call(
        _final_norm_kernel,
        grid=(S // tm,),
        in_specs=[pl.BlockSpec((tm, D), lambda i: (i, 0)), pl.BlockSpec((1, D), lambda i: (0, 0))],
        out_specs=pl.BlockSpec((tm, D), lambda i: (i, 0)),
        out_shape=jax.ShapeDtypeStruct((S, D), F32),
        compiler_params=_params("arbitrary"),
        name="final_norm",
    )(x, w)


def _tile(n, pref):
    t = min(n, pref)
    assert n % t == 0, (n, t)
    return t


def kernel(x, norm_w, w_in, b_in, qk_conv_w, mlstm_norm_w, w_a, dw_w, dw_b, ln_w, ln_b, w_b, w_out, final_norm_w):
    B, S, D = x.shape
    depth = norm_w.shape[0]
    W = w_a.shape[1]
    n_in = w_in.shape[2]
    heads = (n_in - 10 * W) // 4
    dh = W // heads
    assert B == 1 and W == D and w_b.shape[1] == W and n_in == 10 * W + 4 * heads
    assert S % CHUNK == 0 and dh % LANES == 0 and 2 * heads <= LANES
    g0 = 5 * W

    tm = _tile(S, 1024)
    tn = _tile(W, 1024)
    cw = _tile(W, 256)
    tr = _tile(S, 256)

    Q, K_, V, O, ZA, GLA, GLB, ZB, GA, GB = range(10)

    xs = x.reshape(S, D)
    for l in range(depth):
        wl = w_in[l]
        w_main = jnp.concatenate([wl[:, :g0], wl[:, g0 + 4 * heads:]], axis=1).astype(BF16)
        b_main = jnp.concatenate([b_in[l, :g0], b_in[l, g0 + 4 * heads:]])[None, :]
        gcols = lambda a: jnp.concatenate([a[..., g0:g0 + heads], a[..., g0 + 2 * heads:g0 + 3 * heads]], axis=-1)
        fcols = lambda a: jnp.concatenate([a[..., g0 + heads:g0 + 2 * heads], a[..., g0 + 3 * heads:g0 + 4 * heads]], axis=-1)
        padl = lambda a: jnp.pad(a, [(0, 0)] * (a.ndim - 1) + [(0, LANES - 2 * heads)])
        w_g = jnp.concatenate([padl(gcols(wl)), padl(fcols(wl))], axis=1).astype(BF16)
        b_g = jnp.concatenate([padl(gcols(b_in[l])), padl(fcols(b_in[l]))])[None, :]

        u, gates = _inproj(xs, norm_w[l][None, :], w_main, b_main, w_g, b_g, tm=tm, tn=tn)

        zero_b = jnp.zeros((1, W), F32)
        ident = lambda a: a
        qc = _stripe_conv([u], [Q], qk_conv_w[l][:, :W], zero_b, ncols=W, cw=cw, pre=ident, post=_silu,
                          transpose_out=False, scale=1.0, out_dtype=BF16, name="q_conv")
        kt = _stripe_conv([u], [K_], qk_conv_w[l][:, W:], zero_b, ncols=W, cw=cw, pre=ident, post=_silu,
                          transpose_out=True, scale=dh ** -0.5, out_dtype=F32, name="k_conv")
        prep = _gateprep(gates, heads)
        hf, hb = _mlstm(qc, kt, u, prep, heads=heads, v_block=V)
        ya = _post_a(hf, hb, u, mlstm_norm_w[l][None, :], heads=heads, o_block=O, z_block=ZA, tm=tr)

        cconv = _stripe_conv([u, u], [GLA, GLB], dw_w[l], dw_b[l][None, :], ncols=W, cw=_tile(W, LANES),
                             pre=lambda a, b: a * _sigmoid(b), post=ident, transpose_out=False, scale=1.0,
                             out_dtype=F32, name="glu_conv")
        yb = _post_b(cconv, u, ln_w[l][None, :], ln_b[l][None, :], z_block=ZB, tm=tr)

        merged = _merge(ya, yb, w_a[l].astype(BF16), w_b[l].astype(BF16), u, ga_block=GA, gb_block=GB,
                        tm=tm, tn=_tile(D, 512))
        xs = _outproj(merged, w_out[l].astype(BF16), xs, tm=tm, tn=tn)

    return _final_norm(xs, final_norm_w[None, :], tm=tr).reshape(B, S, D)
```

```python
import functools

import jax
import jax.numpy as jnp
from jax import lax
from jax.experimental import pallas as pl
from jax.experimental.pallas import tpu as pltpu

F32 = jnp.float32
BF16 = jnp.bfloat16

EPS = 1e-6
LANES = 128
SUBLANES = 8
MXU_DIM = 256
CHUNK = MXU_DIM
VMEM_LIMIT_BYTES = 56 * 2**20
CONV_PAD = 16
CONV_ROWS = 128
HEADS_PER_STEP = 2
CHUNKS_PER_STEP = 2


def _params(*sem):
    return pltpu.CompilerParams(dimension_semantics=sem, vmem_limit_bytes=VMEM_LIMIT_BYTES)


def _sigmoid(x):
    return 1.0 / (1.0 + jnp.exp(-x))


def _silu(x):
    return x * _sigmoid(x)


def _inproj_kernel(x_ref, nw_ref, w_ref, b_ref, wg_ref, bg_ref, u_ref, v_ref, g_ref, hn_ref, *,
                   row_step, v_lo, v_hi, tiles_per_block, sigmoid_blocks, silu_blocks):
    j = pl.program_id(1)
    blk = j // tiles_per_block

    @pl.when(j == 0)
    def _():
        for r in range(0, x_ref.shape[0], row_step):
            x = x_ref[r:r + row_step, :]
            ms = jnp.mean(x * x, axis=-1, keepdims=True)
            hn_ref[r:r + row_step, :] = ((x * lax.rsqrt(ms + EPS)) * nw_ref[...]).astype(BF16)
        g_ref[...] = jnp.dot(hn_ref[...], wg_ref[...], preferred_element_type=F32) + bg_ref[...]

    is_v = jnp.logical_and(j >= v_lo, j < v_hi)

    @pl.when(is_v)
    def _():
        v_ref[...] = (jnp.dot(hn_ref[...], w_ref[...], preferred_element_type=F32) + b_ref[...]).astype(BF16)

    @pl.when(jnp.logical_not(is_v))
    def _():
        y = jnp.dot(hn_ref[...], w_ref[...], preferred_element_type=F32) + b_ref[...]
        sg = _sigmoid(y)
        is_sig = functools.reduce(jnp.logical_or, [blk == k for k in sigmoid_blocks])
        is_silu = functools.reduce(jnp.logical_or, [blk == k for k in silu_blocks])
        u_ref[...] = jnp.where(is_sig, sg, jnp.where(is_silu, y * sg, y))


def _inproj(x, nw, w, b, wg, bg, *, layer, width, v_block, sigmoid_blocks, silu_blocks, tm, tn):
    S, D = x.shape
    N = w.shape[2]
    NG = wg.shape[2]
    p = width // tn
    v_lo, v_hi = v_block * p, (v_block + 1) * p

    def u_idx(i, j):
        return i, jnp.where(j < v_lo, j, jnp.where(j < v_hi, v_lo - 1, j - p))

    def v_idx(i, j):
        return i, jnp.clip(j - v_lo, 0, p - 1)

    return pl.pallas_call(
        functools.partial(_inproj_kernel, row_step=min(tm, 128), v_lo=v_lo, v_hi=v_hi, tiles_per_block=p,
                          sigmoid_blocks=sigmoid_blocks, silu_blocks=silu_blocks),
        grid=(S // tm, N // tn),
        in_specs=[pl.BlockSpec((tm, D), lambda i, j: (i, 0)),
                  pl.BlockSpec((None, 1, D), lambda i, j: (layer, 0, 0)),
                  pl.BlockSpec((None, D, tn), lambda i, j: (layer, 0, j)),
                  pl.BlockSpec((None, 1, tn), lambda i, j: (layer, 0, j)),
                  pl.BlockSpec((None, D, NG), lambda i, j: (layer, 0, 0)),
                  pl.BlockSpec((None, 1, NG), lambda i, j: (layer, 0, 0))],
        out_specs=[pl.BlockSpec((tm, tn), u_idx),
                   pl.BlockSpec((tm, tn), v_idx),
                   pl.BlockSpec((tm, NG), lambda i, j: (i, 0))],
        out_shape=[jax.ShapeDtypeStruct((S, N - width), F32),
                   jax.ShapeDtypeStruct((S, width), BF16),
                   jax.ShapeDtypeStruct((S, NG), F32)],
        scratch_shapes=[pltpu.VMEM((tm, D), BF16)],
        compiler_params=_params("arbitrary", "arbitrary"),
        name="inproj",
    )(x, nw, w, b, wg, bg)


def _stripe_conv_kernel(*refs, n_in, ktaps, pre, post, transpose_out, scale):
    in_refs = refs[:n_in]
    w_ref, b_ref, o_ref, pad_ref = refs[n_in:n_in + 4]
    S, cw = in_refs[0].shape
    P, R = CONV_PAD, CONV_ROWS
    win_rows = R + 2 * P
    half = ktaps // 2

    pad_ref[0:P, :] = jnp.zeros((P, cw), F32)
    pad_ref[P + S:P + S + P, :] = jnp.zeros((P, cw), F32)

    def fill(i, c):
        r0 = pl.multiple_of(i * R, R)
        vals = [r[pl.ds(r0, R), :] for r in in_refs]
        pad_ref[pl.ds(P + r0, R), :] = pre(*vals)
        return c

    lax.fori_loop(0, S // R, fill, 0)

    def body(i, c):
        r0 = pl.multiple_of(i * R, R)
        win = pad_ref[pl.ds(r0, win_rows), :]
        acc = jnp.zeros((R, cw), F32) + b_ref[...]
        for sub in range(SUBLANES):
            taps = [d for d in range(ktaps) if (P + d - half) % SUBLANES == sub]
            if not taps:
                continue
            wsub = win if sub == 0 else pltpu.roll(win, win_rows - sub, 0)
            for d in taps:
                a = (P + d - half) // SUBLANES
                acc = acc + w_ref[d:d + 1, :] * wsub[SUBLANES * a:SUBLANES * a + R, :]
        y = post(acc)
        if scale != 1.0:
            y = y * scale
        if transpose_out:
            o_ref[:, pl.ds(r0, R)] = y.T.astype(o_ref.dtype)
        else:
            o_ref[pl.ds(r0, R), :] = y.astype(o_ref.dtype)
        return c

    lax.fori_loop(0, S // R, body, 0)


def _stripe_conv(inputs, col_blocks, w, b, *, layer, w_off, ncols, cw, pre, post, transpose_out, scale,
                 out_dtype, name):
    S = inputs[0].shape[0]
    ktaps = w.shape[1]
    nj = ncols // cw
    in_specs = [pl.BlockSpec((S, cw), functools.partial(lambda j, off: (0, off + j), off=cb * nj))
                for cb in col_blocks]
    in_specs += [pl.BlockSpec((None, ktaps, cw), lambda j: (layer, 0, w_off * nj + j)),
                 pl.BlockSpec((None, 1, cw), lambda j: (layer, 0, j))]
    if transpose_out:
        out_spec = pl.BlockSpec((cw, S), lambda j: (j, 0))
        out_shape = jax.ShapeDtypeStruct((ncols, S), out_dtype)
    else:
        out_spec = pl.BlockSpec((S, cw), lambda j: (0, j))
        out_shape = jax.ShapeDtypeStruct((S, ncols), out_dtype)
    return pl.pallas_call(
        functools.partial(_stripe_conv_kernel, n_in=len(inputs), ktaps=ktaps, pre=pre, post=post,
                          transpose_out=transpose_out, scale=scale),
        grid=(nj,),
        in_specs=in_specs,
        out_specs=out_spec,
        out_shape=out_shape,
        scratch_shapes=[pltpu.VMEM((S + 2 * CONV_PAD, cw), F32)],
        compiler_params=_params("arbitrary"),
        name=name,
    )(*inputs, w, b)


def _split3(x):
    a = x.astype(BF16)
    r = x - a.astype(F32)
    b = r.astype(BF16)
    c = (r - b.astype(F32)).astype(BF16)
    return a, b, c


def _gateprep_kernel(gf_ref, gb_ref, *out_and_scratch, heads):
    outs = out_and_scratch[:10]
    m_ref = out_and_scratch[10]
    L = CHUNK

    @pl.when(pl.program_id(0) == 0)
    def _():
        m_ref[...] = jnp.zeros(m_ref.shape, F32)

    row = lax.broadcasted_iota(jnp.int32, (L, L), 0)
    col = lax.broadcasted_iota(jnp.int32, (L, L), 1)
    for d, g_ref in enumerate((gf_ref, gb_ref)):
        wi_ref, wr_ref, er_ref, wk_ref, dc_ref = outs[5 * d:5 * d + 5]
        keep = (col <= row) if d == 0 else (col >= row)
        tri = keep.astype(BF16)
        ti = g_ref[:, 0:LANES]
        tf = g_ref[:, LANES:2 * LANES]
        ls = jnp.minimum(tf, 0.0) - jnp.log1p(jnp.exp(-jnp.abs(tf)))
        g = sum(jnp.dot(tri, p, preferred_element_type=F32) for p in _split3(ls))
        gT = g.T
        iT = ti.T
        for h in range(heads):
            c = heads * d + h
            m = m_ref[c:c + 1, :]
            g_col = g[:, c:c + 1]
            g_row = gT[c:c + 1, :]
            i_row = iT[c:c + 1, :]
            a = g_col + m
            dmat = jnp.where(keep, g_col - g_row + i_row, -jnp.inf)
            dmax = jnp.max(dmat, axis=-1, keepdims=True)
            m_rows = jnp.maximum(a, dmax)
            wi_ref[h] = jnp.exp(dmat - jnp.maximum(a[:, 0:1], dmax))
            wr_ref[h] = jnp.exp(a - m_rows)
            er_ref[h] = jnp.exp(-m_rows)
            gtot = g_row[:, L - 1:L] if d == 0 else g_row[:, 0:1]
            b_row = gtot - g_row + i_row
            m_new = jnp.maximum(gtot + m, jnp.max(b_row, axis=-1, keepdims=True))
            wk_ref[h] = jnp.broadcast_to(jnp.exp(b_row - m_new[:, 0:1]), (SUBLANES, L))
            dc_ref[h] = jnp.broadcast_to(jnp.exp(gtot + m - m_new), (SUBLANES, LANES))
            m_ref[c:c + 1, :] = m_new


def _gateprep(gates, heads):
    S = gates.shape[0]
    L = CHUNK
    nc = S // L
    fw = lambda j: (0, j, 0)
    bw = lambda j: (0, nc - 1 - j, 0)
    out_specs, out_shape = [], []
    for idx in (fw, bw):
        out_specs += [pl.BlockSpec((heads, L, L), idx), pl.BlockSpec((heads, L, LANES), idx),
                      pl.BlockSpec((heads, L, LANES), idx), pl.BlockSpec((heads, SUBLANES, L), idx),
                      pl.BlockSpec((heads, SUBLANES, LANES), idx)]
        out_shape += [jax.ShapeDtypeStruct((heads, S, L), F32), jax.ShapeDtypeStruct((heads, S, LANES), F32),
                      jax.ShapeDtypeStruct((heads, S, LANES), F32),
                      jax.ShapeDtypeStruct((heads, nc * SUBLANES, L), F32),
                      jax.ShapeDtypeStruct((heads, nc * SUBLANES, LANES), F32)]
    return pl.pallas_call(
        functools.partial(_gateprep_kernel, heads=heads),
        grid=(nc,),
        in_specs=[pl.BlockSpec((L, 2 * LANES), lambda j: (j, 0)),
                  pl.BlockSpec((L, 2 * LANES), lambda j: (nc - 1 - j, 0))],
        out_specs=out_specs,
        out_shape=out_shape,
        scratch_shapes=[pltpu.VMEM((2 * heads, LANES), F32)],
        compiler_params=_params("arbitrary"),
        name="gateprep",
    )(gates, gates)


def _mlstm_kernel(*refs, reverse, epilogue, dh):
    q_ref, kt_ref, v_ref, wi_ref, wr_ref, er_ref, wk_ref, dc_ref = refs[:8]
    if epilogue:
        hb_ref, o_ref, z_ref, nw_ref = refs[8:12]
        out_ref, c_ref, cb_ref, n_ref = refs[12:]
    else:
        out_ref, c_ref, cb_ref, n_ref = refs[8:]
    L = CHUNK
    nlb = dh // LANES

    @pl.when(pl.program_id(1) == 0)
    def _():
        c_ref[...] = jnp.zeros(c_ref.shape, F32)
        cb_ref[...] = jnp.zeros(cb_ref.shape, BF16)
        n_ref[...] = jnp.zeros(n_ref.shape, F32)

    ones = jnp.ones((L, LANES), BF16)
    chunks = range(CHUNKS_PER_STEP)
    for t in (reversed(chunks) if reverse else chunks):
        rows = slice(t * L, (t + 1) * L)
        srow = slice(t * SUBLANES, t * SUBLANES + 1)
        for hh in range(HEADS_PER_STEP):
            cols = slice(hh * dh, (hh + 1) * dh)
            q = q_ref[rows, cols]
            kt = kt_ref[cols, rows]
            v1 = jnp.concatenate([v_ref[rows, cols], ones], axis=1)
            nvec = n_ref[hh]
            rhs = jnp.concatenate([kt.astype(BF16), nvec.astype(BF16)], axis=1)
            qk2 = jnp.dot(q, rhs, preferred_element_type=F32)
            s = qk2[:, :L] * wi_ref[hh, rows, :]
            w_inter = wr_ref[hh, rows, :]
            sv = jnp.dot(s.astype(BF16), v1, preferred_element_type=F32)
            den = sv[:, dh:] + w_inter * qk2[:, L:]
            rden = 1.0 / jnp.maximum(jnp.abs(den), er_ref[hh, rows, :])
            qc = jnp.dot(q, cb_ref[hh], preferred_element_type=F32)
            hblk = [(sv[:, b * LANES:(b + 1) * LANES] + w_inter * qc[:, b * LANES:(b + 1) * LANES]) * rden
                    for b in range(nlb)]
            if epilogue:
                ht = []
                for b in range(nlb):
                    cb = slice(hh * dh + b * LANES, hh * dh + (b + 1) * LANES)
                    ht.append(o_ref[rows, cb] * (hblk[b] + hb_ref[rows, cb]))
                ssq = sum(jnp.sum(x * x, axis=-1, keepdims=True) for x in ht)
                rinv = lax.rsqrt(ssq * (1.0 / dh) + EPS)
                for b in range(nlb):
                    cb = slice(hh * dh + b * LANES, hh * dh + (b + 1) * LANES)
                    y = ((ht[b] * rinv) * nw_ref[:, cb]) * z_ref[rows, cb]
                    out_ref[rows, cb] = y.astype(out_ref.dtype)
            else:
                for b in range(nlb):
                    cb = slice(hh * dh + b * LANES, hh * dh + (b + 1) * LANES)
                    out_ref[rows, cb] = hblk[b]
            wkt = kt * wk_ref[hh, srow, :]
            dec = dc_ref[hh, srow, :]
            dcm = jnp.dot(wkt.astype(BF16), v1, preferred_element_type=F32)
            n_ref[hh] = dec * nvec + dcm[:, dh:]
            cnew = jnp.concatenate([dec] * nlb, axis=1) * c_ref[hh] + dcm[:, :dh]
            c_ref[hh] = cnew
            cb_ref[hh] = cnew.astype(BF16)


def _mlstm(qc, kt, v, prep, *, heads, reverse, epilogue_args=None):
    S, W = qc.shape
    dh = W // heads
    L = CHUNK
    T, HP = CHUNKS_PER_STEP, HEADS_PER_STEP
    ng = S // (T * L)
    wi, wr, er, wk, dc = prep
    g = (lambda j: ng - 1 - j) if reverse else (lambda j: j)
    rowblk = pl.BlockSpec((T * L, HP * dh), lambda h, j: (g(j), h))
    in_specs = [rowblk,
                pl.BlockSpec((HP * dh, T * L), lambda h, j: (h, g(j))),
                rowblk,
                pl.BlockSpec((HP, T * L, L), lambda h, j: (h, g(j), 0)),
                pl.BlockSpec((HP, T * L, LANES), lambda h, j: (h, g(j), 0)),
                pl.BlockSpec((HP, T * L, LANES), lambda h, j: (h, g(j), 0)),
                pl.BlockSpec((HP, T * SUBLANES, L), lambda h, j: (h, g(j), 0)),
                pl.BlockSpec((HP, T * SUBLANES, LANES), lambda h, j: (h, g(j), 0))]
    args = [qc, kt, v, wi, wr, er, wk, dc]
    if epilogue_args is not None:
        hb, u, o_block, z_block, nw, layer = epilogue_args
        nhp = heads // HP
        in_specs += [rowblk,
                     pl.BlockSpec((T * L, HP * dh), lambda h, j: (g(j), o_block * nhp + h)),
                     pl.BlockSpec((T * L, HP * dh), lambda h, j: (g(j), z_block * nhp + h)),
                     pl.BlockSpec((None, 1, HP * dh), lambda h, j: (layer, 0, h))]
        args += [hb, u, u, nw]
        out_dtype = BF16
    else:
        out_dtype = F32
    return pl.pallas_call(
        functools.partial(_mlstm_kernel, reverse=reverse, epilogue=epilogue_args is not None, dh=dh),
        grid=(heads // HP, ng),
        in_specs=in_specs,
        out_specs=rowblk,
        out_shape=jax.ShapeDtypeStruct((S, W), out_dtype),
        scratch_shapes=[pltpu.VMEM((HP, dh, dh), F32), pltpu.VMEM((HP, dh, dh), BF16),
                        pltpu.VMEM((HP, dh, LANES), F32)],
        compiler_params=_params("arbitrary", "arbitrary"),
        name="mlstm_fw" if epilogue_args is not None else "mlstm_bw",
    )(*args)


def _post_b_kernel(c_ref, z_ref, lw_ref, lb_ref, y_ref):
    c = c_ref[...]
    mu = jnp.mean(c, axis=-1, keepdims=True)
    cc = c - mu
    var = jnp.mean(cc * cc, axis=-1, keepdims=True)
    y = cc * lax.rsqrt(var + EPS) * lw_ref[...] + lb_ref[...]
    y_ref[...] = (_silu(y) * z_ref[...]).astype(y_ref.dtype)


def _post_b(c, u, lw, lb, *, layer, z_block, tm):
    S, W = c.shape
    return pl.pallas_call(
        _post_b_kernel,
        grid=(S // tm,),
        in_specs=[pl.BlockSpec((tm, W), lambda i: (i, 0)),
                  pl.BlockSpec((tm, W), lambda i: (i, z_block)),
                  pl.BlockSpec((None, 1, W), lambda i: (layer, 0, 0)),
                  pl.BlockSpec((None, 1, W), lambda i: (layer, 0, 0))],
        out_specs=pl.BlockSpec((tm, W), lambda i: (i, 0)),
        out_shape=jax.ShapeDtypeStruct((S, W), BF16),
        compiler_params=_params("arbitrary"),
        name="post_b",
    )(c, u, lw, lb)


def _merge_kernel(ya_ref, yb_ref, wa_ref, wb_ref, ga_ref, gb_ref, m_ref):
    pa = jnp.dot(ya_ref[...], wa_ref[...], preferred_element_type=F32)
    pb = jnp.dot(yb_ref[...], wb_ref[...], preferred_element_type=F32)
    m_ref[...] = (ga_ref[...] * pa + gb_ref[...] * pb).astype(m_ref.dtype)


def _merge(ya, yb, wa, wb, u, *, layer, ga_block, gb_block, tm, tn):
    S, W = ya.shape
    D = wa.shape[2]
    nj = D // tn
    return pl.pallas_call(
        _merge_kernel,
        grid=(S // tm, nj),
        in_specs=[pl.BlockSpec((tm, W), lambda i, j: (i, 0)),
                  pl.BlockSpec((tm, W), lambda i, j: (i, 0)),
                  pl.BlockSpec((None, W, tn), lambda i, j: (layer, 0, j)),
                  pl.BlockSpec((None, W, tn), lambda i, j: (layer, 0, j)),
                  pl.BlockSpec((tm, tn), lambda i, j: (i, ga_block * nj + j)),
                  pl.BlockSpec((tm, tn), lambda i, j: (i, gb_block * nj + j))],
        out_specs=pl.BlockSpec((tm, tn), lambda i, j: (i, j)),
        out_shape=jax.ShapeDtypeStruct((S, D), BF16),
        compiler_params=_params("arbitrary", "arbitrary"),
        name="merge",
    )(ya, yb, wa, wb, u, u)


def _outproj_kernel(m_ref, w_ref, x_ref, o_ref):
    o_ref[...] = x_ref[...] + jnp.dot(m_ref[...], w_ref[...], preferred_element_type=F32)


def _outproj(m, w, x, *, layer, tm, tn):
    S, D = x.shape
    return pl.pallas_call(
        _outproj_kernel,
        grid=(S // tm, D // tn),
        in_specs=[pl.BlockSpec((tm, D), lambda i, j: (i, 0)),
                  pl.BlockSpec((None, D, tn), lambda i, j: (layer, 0, j)),
                  pl.BlockSpec((tm, tn), lambda i, j: (i, j))],
        out_specs=pl.BlockSpec((tm, tn), lambda i, j: (i, j)),
        out_shape=jax.ShapeDtypeStruct((S, D), F32),
        compiler_params=_params("arbitrary", "arbitrary"),
        name="outproj",
    )(m, w, x)


def _final_norm_kernel(x_ref, w_ref, o_ref):
    x = x_ref[...]
    o_ref[...] = (x * lax.rsqrt(jnp.mean(x * x, axis=-1, keepdims=True) + EPS)) * w_ref[...]


def _final_norm(x, w, *, tm):
    S, D = x.shape
    return pl.pallas_call(
        _final_norm_kernel,
        grid=(S // tm,),
        in_specs=[pl.BlockSpec((tm, D), lambda i: (i, 0)), pl.BlockSpec((1, D), lambda i: (0, 0))],
        out_specs=pl.BlockSpec((tm, D), lambda i: (i, 0)),
        out_shape=jax.ShapeDtypeStruct((S, D), F32),
        compiler_params=_params("arbitrary"),
        name="final_norm",
    )(x, w)


def _tile(n, pref):
    t = min(n, pref)
    assert n % t == 0, (n, t)
    return t


def kernel(x, norm_w, w_in, b_in, qk_conv_w, mlstm_norm_w, w_a, dw_w, dw_b, ln_w, ln_b, w_b, w_out, final_norm_w):
    B, S, D = x.shape
    depth = norm_w.shape[0]
    W = w_a.shape[1]
    n_in = w_in.shape[2]
    heads = (n_in - 10 * W) // 4
    dh = W // heads
    assert B == 1 and W == D and w_b.shape[1] == W and n_in == 10 * W + 4 * heads
    assert S % (CHUNK * CHUNKS_PER_STEP) == 0 and dh % LANES == 0 and 2 * heads <= LANES
    assert heads % HEADS_PER_STEP == 0
    g0 = 5 * W
    ng = 4 * heads

    tm = _tile(S, 1024)
    tn = _tile(W, 1024)
    cw = _tile(W, 256)
    tr = _tile(S, 256)

    V = 2
    Q, K_, O, ZA, GLA, GLB, ZB, GA, GB = range(9)

    w_main = jnp.concatenate([w_in[:, :, :g0], w_in[:, :, g0 + ng:]], axis=2).astype(BF16)
    b_main = jnp.concatenate([b_in[:, :g0], b_in[:, g0 + ng:]], axis=1)[:, None, :]
    icols = lambda a: jnp.concatenate([a[..., g0:g0 + heads], a[..., g0 + 2 * heads:g0 + 3 * heads]], axis=-1)
    fcols = lambda a: jnp.concatenate([a[..., g0 + heads:g0 + 2 * heads], a[..., g0 + 3 * heads:g0 + ng]], axis=-1)
    padl = lambda a: jnp.pad(a, [(0, 0)] * (a.ndim - 1) + [(0, LANES - 2 * heads)])
    w_g = jnp.concatenate([padl(icols(w_in)), padl(fcols(w_in))], axis=-1).astype(BF16)
    b_g = jnp.concatenate([padl(icols(b_in)), padl(fcols(b_in))], axis=-1)[:, None, :]
    wa16, wb16, wo16 = w_a.astype(BF16), w_b.astype(BF16), w_out.astype(BF16)
    row3 = lambda a: a[:, None, :]
    zero_b = jnp.zeros((depth, 1, W), F32)
    ident = lambda a: a

    xs = x.reshape(S, D)
    for l in range(depth):
        u, v, gates = _inproj(xs, row3(norm_w), w_main, b_main, w_g, b_g, layer=l, width=W, v_block=V,
                              sigmoid_blocks=(3, 6, 8, 9), silu_blocks=(4, 7), tm=tm, tn=tn)
        qc = _stripe_conv([u], [Q], qk_conv_w, zero_b, layer=l, w_off=0, ncols=W, cw=cw, pre=ident, post=_silu,
                          transpose_out=False, scale=1.0, out_dtype=BF16, name="q_conv")
        kt = _stripe_conv([u], [K_], qk_conv_w, zero_b, layer=l, w_off=1, ncols=W, cw=cw, pre=ident, post=_silu,
                          transpose_out=True, scale=dh ** -0.5, out_dtype=F32, name="k_conv")
        prep = _gateprep(gates, heads)
        hb = _mlstm(qc, kt, v, prep[5:], heads=heads, reverse=True)
        ya = _mlstm(qc, kt, v, prep[:5], heads=heads, reverse=False,
                    epilogue_args=(hb, u, O, ZA, row3(mlstm_norm_w), l))

        cconv = _stripe_conv([u, u], [GLA, GLB], dw_w, row3(dw_b), layer=l, w_off=0, ncols=W,
                             cw=_tile(W, LANES), pre=lambda a, b: a * b, post=ident,
                             transpose_out=False, scale=1.0, out_dtype=F32, name="glu_conv")
        yb = _post_b(cconv, u, row3(ln_w), row3(ln_b), layer=l, z_block=ZB, tm=tr)

        merged = _merge(ya, yb, wa16, wb16, u, layer=l, ga_block=GA, gb_block=GB, tm=tm, tn=_tile(D, 512))
        xs = _outproj(merged, wo16, xs, layer=l, tm=tm, tn=tn)

    return _final_norm(xs, final_norm_w[None, :], tm=tr).reshape(B, S, D)
```

```python
import functools

import jax
import jax.numpy as jnp
from jax import lax
from jax.experimental import pallas as pl
from jax.experimental.pallas import tpu as pltpu

F32 = jnp.float32
BF16 = jnp.bfloat16

EPS = 1e-6
LANES = 128
SUBLANES = 8
MXU_DIM = 256
CHUNK = MXU_DIM
VMEM_LIMIT_BYTES = 56 * 2**20
CONV_PAD = 16
CONV_ROWS = 128
HEADS_PER_STEP = 2
CHUNKS_PER_STEP = 2


def _params(*sem):
    return pltpu.CompilerParams(dimension_semantics=sem, vmem_limit_bytes=VMEM_LIMIT_BYTES)


def _sigmoid(x):
    return 0.5 * jnp.tanh(0.5 * x) + 0.5


def _silu(x):
    return x * _sigmoid(x)


def _wprep_kernel(a_ref, nxt_ref, o_ref, *, first_shifted, shift):
    c = pl.program_id(2)

    @pl.when(c < first_shifted)
    def _():
        o_ref[...] = a_ref[...].astype(BF16)

    @pl.when(c >= first_shifted)
    def _():
        o_ref[...] = jnp.concatenate([a_ref[:, shift:], nxt_ref[:, :shift]], axis=1).astype(BF16)


def _wprep(w, *, gate_start, gate_width, tk, tn):
    depth, D, N = w.shape
    assert gate_start % tn == 0 and (N - gate_width) % tn == 0 and gate_width < LANES
    return pl.pallas_call(
        functools.partial(_wprep_kernel, first_shifted=gate_start // tn, shift=gate_width),
        grid=(depth, D // tk, (N - gate_width) // tn),
        in_specs=[pl.BlockSpec((None, tk, tn), lambda l, r, c: (l, r, c)),
                  pl.BlockSpec((None, tk, LANES), lambda l, r, c: (l, r, (c + 1) * (tn // LANES)))],
        out_specs=pl.BlockSpec((None, tk, tn), lambda l, r, c: (l, r, c)),
        out_shape=jax.ShapeDtypeStruct((depth, D, N - gate_width), BF16),
        compiler_params=_params("arbitrary", "arbitrary", "arbitrary"),
        name="wprep",
    )(w, w)


def _inproj_kernel(x_ref, nw_ref, w_ref, b_ref, wg_ref, bg_ref, u_ref, v_ref, g_ref, hn_ref, *,
                   row_step, col_step, tiles_per_block, v_block, sigmoid_blocks, silu_blocks):
    j = pl.program_id(1)
    blk = j // tiles_per_block

    @pl.when(j == 0)
    def _():
        for r in range(0, x_ref.shape[0], row_step):
            x = x_ref[r:r + row_step, :]
            ms = jnp.mean(x * x, axis=-1, keepdims=True)
            hn_ref[r:r + row_step, :] = ((x * lax.rsqrt(ms + EPS)) * nw_ref[...]).astype(BF16)
        g_ref[...] = jnp.dot(hn_ref[...], wg_ref[...], preferred_element_type=F32) + bg_ref[...]

    def emit(out_ref, act):
        for c in range(0, w_ref.shape[1], col_step):
            y = jnp.dot(hn_ref[...], w_ref[:, c:c + col_step], preferred_element_type=F32) + b_ref[:, c:c + col_step]
            out_ref[:, c:c + col_step] = act(y).astype(out_ref.dtype)

    any_of = lambda blocks: functools.reduce(jnp.logical_or, [blk == k for k in blocks])
    is_v, is_sig, is_silu = blk == v_block, any_of(sigmoid_blocks), any_of(silu_blocks)
    is_plain = jnp.logical_not(jnp.logical_or(is_v, jnp.logical_or(is_sig, is_silu)))
    pl.when(is_v)(lambda: emit(v_ref, lambda y: y))
    pl.when(is_plain)(lambda: emit(u_ref, lambda y: y))
    pl.when(is_sig)(lambda: emit(u_ref, _sigmoid))
    pl.when(is_silu)(lambda: emit(u_ref, _silu))


def _inproj(x, nw, w, b, wg, bg, *, layer, width, v_block, sigmoid_blocks, silu_blocks, tm, tn):
    S, D = x.shape
    N = w.shape[2]
    NG = wg.shape[2]
    p = width // tn
    v_lo, v_hi = v_block * p, (v_block + 1) * p

    def u_idx(i, j):
        return i, jnp.where(j < v_lo, j, jnp.where(j < v_hi, v_lo - 1, j - p))

    def v_idx(i, j):
        return i, jnp.clip(j - v_lo, 0, p - 1)

    return pl.pallas_call(
        functools.partial(_inproj_kernel, row_step=min(tm, 128), col_step=min(tn, MXU_DIM), tiles_per_block=p,
                          v_block=v_block, sigmoid_blocks=sigmoid_blocks, silu_blocks=silu_blocks),
        grid=(S // tm, N // tn),
        in_specs=[pl.BlockSpec((tm, D), lambda i, j: (i, 0)),
                  pl.BlockSpec((None, 1, D), lambda i, j: (layer, 0, 0)),
                  pl.BlockSpec((None, D, tn), lambda i, j: (layer, 0, j)),
                  pl.BlockSpec((None, 1, tn), lambda i, j: (layer, 0, j)),
                  pl.BlockSpec((None, D, NG), lambda i, j: (layer, 0, 0)),
                  pl.BlockSpec((None, 1, NG), lambda i, j: (layer, 0, 0))],
        out_specs=[pl.BlockSpec((tm, tn), u_idx),
                   pl.BlockSpec((tm, tn), v_idx),
                   pl.BlockSpec((tm, NG), lambda i, j: (i, 0))],
        out_shape=[jax.ShapeDtypeStruct((S, N - width), F32),
                   jax.ShapeDtypeStruct((S, width), BF16),
                   jax.ShapeDtypeStruct((S, NG), F32)],
        scratch_shapes=[pltpu.VMEM((tm, D), BF16)],
        compiler_params=_params("arbitrary", "arbitrary"),
        name="inproj",
    )(x, nw, w, b, wg, bg)


def _stripe_conv_kernel(*refs, n_in, ktaps, pre, post, transpose_out, scale):
    in_refs = refs[:n_in]
    w_ref, b_ref, o_ref, pad_ref = refs[n_in:n_in + 4]
    S, cw = in_refs[0].shape
    P, R = CONV_PAD, CONV_ROWS
    win_rows = R + 2 * P
    half = ktaps // 2

    pad_ref[0:P, :] = jnp.zeros((P, cw), F32)
    pad_ref[P + S:P + S + P, :] = jnp.zeros((P, cw), F32)

    def fill(i, c):
        r0 = pl.multiple_of(i * R, R)
        vals = [r[pl.ds(r0, R), :] for r in in_refs]
        pad_ref[pl.ds(P + r0, R), :] = pre(*vals)
        return c

    lax.fori_loop(0, S // R, fill, 0)

    def body(i, c):
        r0 = pl.multiple_of(i * R, R)
        win = pad_ref[pl.ds(r0, win_rows), :]
        acc = jnp.zeros((R, cw), F32) + b_ref[...]
        for sub in range(SUBLANES):
            taps = [d for d in range(ktaps) if (P + d - half) % SUBLANES == sub]
            if not taps:
                continue
            wsub = win if sub == 0 else pltpu.roll(win, win_rows - sub, 0)
            for d in taps:
                a = (P + d - half) // SUBLANES
                acc = acc + w_ref[d:d + 1, :] * wsub[SUBLANES * a:SUBLANES * a + R, :]
        y = post(acc)
        if scale != 1.0:
            y = y * scale
        if transpose_out:
            o_ref[:, pl.ds(r0, R)] = y.T.astype(o_ref.dtype)
        else:
            o_ref[pl.ds(r0, R), :] = y.astype(o_ref.dtype)
        return c

    lax.fori_loop(0, S // R, body, 0)


def _stripe_conv(inputs, col_blocks, w, b, *, layer, w_off, ncols, cw, pre, post, transpose_out, scale,
                 out_dtype, name):
    S = inputs[0].shape[0]
    ktaps = w.shape[1]
    nj = ncols // cw
    in_specs = [pl.BlockSpec((S, cw), functools.partial(lambda j, off: (0, off + j), off=cb * nj))
                for cb in col_blocks]
    in_specs += [pl.BlockSpec((None, ktaps, cw), lambda j: (layer, 0, w_off * nj + j)),
                 pl.BlockSpec((None, 1, cw), lambda j: (layer, 0, j))]
    if transpose_out:
        out_spec = pl.BlockSpec((cw, S), lambda j: (j, 0))
        out_shape = jax.ShapeDtypeStruct((ncols, S), out_dtype)
    else:
        out_spec = pl.BlockSpec((S, cw), lambda j: (0, j))
        out_shape = jax.ShapeDtypeStruct((S, ncols), out_dtype)
    return pl.pallas_call(
        functools.partial(_stripe_conv_kernel, n_in=len(inputs), ktaps=ktaps, pre=pre, post=post,
                          transpose_out=transpose_out, scale=scale),
        grid=(nj,),
        in_specs=in_specs,
        out_specs=out_spec,
        out_shape=out_shape,
        scratch_shapes=[pltpu.VMEM((S + 2 * CONV_PAD, cw), F32)],
        compiler_params=_params("arbitrary"),
        name=name,
    )(*inputs, w, b)


def _split3(x):
    a = x.astype(BF16)
    r = x - a.astype(F32)
    b = r.astype(BF16)
    c = (r - b.astype(F32)).astype(BF16)
    return a, b, c


def _gateprep_kernel(gf_ref, gb_ref, *out_and_scratch, heads):
    outs = out_and_scratch[:10]
    m_ref = out_and_scratch[10]
    L = CHUNK

    @pl.when(pl.program_id(0) == 0)
    def _():
        m_ref[...] = jnp.zeros(m_ref.shape, F32)

    row = lax.broadcasted_iota(jnp.int32, (L, L), 0)
    col = lax.broadcasted_iota(jnp.int32, (L, L), 1)
    for d, g_ref in enumerate((gf_ref, gb_ref)):
        wi_ref, wr_ref, er_ref, wk_ref, dc_ref = outs[5 * d:5 * d + 5]
        keep = (col <= row) if d == 0 else (col >= row)
        tri = keep.astype(BF16)
        ti = g_ref[:, 0:LANES]
        tf = g_ref[:, LANES:2 * LANES]
        ls = jnp.minimum(tf, 0.0) - jnp.log1p(jnp.exp(-jnp.abs(tf)))
        g = sum(jnp.dot(tri, p, preferred_element_type=F32) for p in _split3(ls))
        gT = g.T
        iT = ti.T
        for h in range(heads):
            c = heads * d + h
            m = m_ref[c:c + 1, :]
            g_col = g[:, c:c + 1]
            g_row = gT[c:c + 1, :]
            i_row = iT[c:c + 1, :]
            a = g_col + m
            dmat = jnp.where(keep, g_col - g_row + i_row, -jnp.inf)
            dmax = jnp.max(dmat, axis=-1, keepdims=True)
            m_rows = jnp.maximum(a, dmax)
            wi_ref[h] = jnp.exp(dmat - jnp.maximum(a[:, 0:1], dmax))
            wr_ref[h] = jnp.exp(a - m_rows)
            er_ref[h] = jnp.exp(-m_rows)
            gtot = g_row[:, L - 1:L] if d == 0 else g_row[:, 0:1]
            b_row = gtot - g_row + i_row
            m_new = jnp.maximum(gtot + m, jnp.max(b_row, axis=-1, keepdims=True))
            wk_ref[h] = jnp.broadcast_to(jnp.exp(b_row - m_new[:, 0:1]), (SUBLANES, L))
            dc_ref[h] = jnp.broadcast_to(jnp.exp(gtot + m - m_new), (SUBLANES, LANES))
            m_ref[c:c + 1, :] = m_new


def _gateprep(gates, heads):
    S = gates.shape[0]
    L = CHUNK
    nc = S // L
    fw = lambda j: (0, j, 0)
    bw = lambda j: (0, nc - 1 - j, 0)
    out_specs, out_shape = [], []
    for idx in (fw, bw):
        out_specs += [pl.BlockSpec((heads, L, L), idx), pl.BlockSpec((heads, L, LANES), idx),
                      pl.BlockSpec((heads, L, LANES), idx), pl.BlockSpec((heads, SUBLANES, L), idx),
                      pl.BlockSpec((heads, SUBLANES, LANES), idx)]
        out_shape += [jax.ShapeDtypeStruct((heads, S, L), F32), jax.ShapeDtypeStruct((heads, S, LANES), F32),
                      jax.ShapeDtypeStruct((heads, S, LANES), F32),
                      jax.ShapeDtypeStruct((heads, nc * SUBLANES, L), F32),
                      jax.ShapeDtypeStruct((heads, nc * SUBLANES, LANES), F32)]
    return pl.pallas_call(
        functools.partial(_gateprep_kernel, heads=heads),
        grid=(nc,),
        in_specs=[pl.BlockSpec((L, 2 * LANES), lambda j: (j, 0)),
                  pl.BlockSpec((L, 2 * LANES), lambda j: (nc - 1 - j, 0))],
        out_specs=out_specs,
        out_shape=out_shape,
        scratch_shapes=[pltpu.VMEM((2 * heads, LANES), F32)],
        compiler_params=_params("arbitrary"),
        name="gateprep",
    )(gates, gates)


def _mlstm_kernel(*refs, reverse, epilogue, dh):
    q_ref, kt_ref, v_ref, wi_ref, wr_ref, er_ref, wk_ref, dc_ref = refs[:8]
    if epilogue:
        hb_ref, o_ref, z_ref, nw_ref = refs[8:12]
        out_ref, c_ref, cb_ref, n_ref = refs[12:]
    else:
        out_ref, c_ref, cb_ref, n_ref = refs[8:]
    L = CHUNK
    nlb = dh // LANES

    @pl.when(pl.program_id(1) == 0)
    def _():
        c_ref[...] = jnp.zeros(c_ref.shape, F32)
        cb_ref[...] = jnp.zeros(cb_ref.shape, BF16)
        n_ref[...] = jnp.zeros(n_ref.shape, F32)

    ones = jnp.ones((L, LANES), BF16)
    chunks = range(CHUNKS_PER_STEP)
    for t in (reversed(chunks) if reverse else chunks):
        rows = slice(t * L, (t + 1) * L)
        srow = slice(t * SUBLANES, t * SUBLANES + 1)
        for hh in range(HEADS_PER_STEP):
            cols = slice(hh * dh, (hh + 1) * dh)
            q = q_ref[rows, cols]
            kt = kt_ref[cols, rows]
            v1 = jnp.concatenate([v_ref[rows, cols], ones], axis=1)
            nvec = n_ref[hh]
            rhs = jnp.concatenate([kt.astype(BF16), nvec.astype(BF16)], axis=1)
            qk2 = jnp.dot(q, rhs, preferred_element_type=F32)
            s = qk2[:, :L] * wi_ref[hh, rows, :]
            w_inter = wr_ref[hh, rows, :]
            sv = jnp.dot(s.astype(BF16), v1, preferred_element_type=F32)
            den = sv[:, dh:] + w_inter * qk2[:, L:]
            rden = 1.0 / jnp.maximum(jnp.abs(den), er_ref[hh, rows, :])
            qc = jnp.dot(q, cb_ref[hh], preferred_element_type=F32)
            hblk = [(sv[:, b * LANES:(b + 1) * LANES] + w_inter * qc[:, b * LANES:(b + 1) * LANES]) * rden
                    for b in range(nlb)]
            if epilogue:
                ht = []
                for b in range(nlb):
                    cb = slice(hh * dh + b * LANES, hh * dh + (b + 1) * LANES)
                    ht.append(o_ref[rows, cb] * (hblk[b] + hb_ref[rows, cb]))
                ssq = sum(jnp.sum(x * x, axis=-1, keepdims=True) for x in ht)
                rinv = lax.rsqrt(ssq * (1.0 / dh) + EPS)
                for b in range(nlb):
                    cb = slice(hh * dh + b * LANES, hh * dh + (b + 1) * LANES)
                    y = ((ht[b] * rinv) * nw_ref[:, cb]) * z_ref[rows, cb]
                    out_ref[rows, cb] = y.astype(out_ref.dtype)
            else:
                for b in range(nlb):
                    cb = slice(hh * dh + b * LANES, hh * dh + (b + 1) * LANES)
                    out_ref[rows, cb] = hblk[b]
            wkt = kt * wk_ref[hh, srow, :]
            dec = dc_ref[hh, srow, :]
            dcm = jnp.dot(wkt.astype(BF16), v1, preferred_element_type=F32)
            n_ref[hh] = dec * nvec + dcm[:, dh:]
            cnew = jnp.concatenate([dec] * nlb, axis=1) * c_ref[hh] + dcm[:, :dh]
            c_ref[hh] = cnew
            cb_ref[hh] = cnew.astype(BF16)


def _mlstm(qc, kt, v, prep, *, heads, reverse, epilogue_args=None):
    S, W = qc.shape
    dh = W // heads
    L = CHUNK
    T, HP = CHUNKS_PER_STEP, HEADS_PER_STEP
    ng = S // (T * L)
    wi, wr, er, wk, dc = prep
    g = (lambda j: ng - 1 - j) if reverse else (lambda j: j)
    rowblk = pl.BlockSpec((T * L, HP * dh), lambda h, j: (g(j), h))
    in_specs = [rowblk,
                pl.BlockSpec((HP * dh, T * L), lambda h, j: (h, g(j))),
                rowblk,
                pl.BlockSpec((HP, T * L, L), lambda h, j: (h, g(j), 0)),
                pl.BlockSpec((HP, T * L, LANES), lambda h, j: (h, g(j), 0)),
                pl.BlockSpec((HP, T * L, LANES), lambda h, j: (h, g(j), 0)),
                pl.BlockSpec((HP, T * SUBLANES, L), lambda h, j: (h, g(j), 0)),
                pl.BlockSpec((HP, T * SUBLANES, LANES), lambda h, j: (h, g(j), 0))]
    args = [qc, kt, v, wi, wr, er, wk, dc]
    if epilogue_args is not None:
        hb, u, o_block, z_block, nw, layer = epilogue_args
        nhp = heads // HP
        in_specs += [rowblk,
                     pl.BlockSpec((T * L, HP * dh), lambda h, j: (g(j), o_block * nhp + h)),
                     pl.BlockSpec((T * L, HP * dh), lambda h, j: (g(j), z_block * nhp + h)),
                     pl.BlockSpec((None, 1, HP * dh), lambda h, j: (layer, 0, h))]
        args += [hb, u, u, nw]
        out_dtype = BF16
    else:
        out_dtype = F32
    return pl.pallas_call(
        functools.partial(_mlstm_kernel, reverse=reverse, epilogue=epilogue_args is not None, dh=dh),
        grid=(heads // HP, ng),
        in_specs=in_specs,
        out_specs=rowblk,
        out_shape=jax.ShapeDtypeStruct((S, W), out_dtype),
        scratch_shapes=[pltpu.VMEM((HP, dh, dh), F32), pltpu.VMEM((HP, dh, dh), BF16),
                        pltpu.VMEM((HP, dh, LANES), F32)],
        compiler_params=_params("arbitrary", "arbitrary"),
        name="mlstm_fw" if epilogue_args is not None else "mlstm_bw",
    )(*args)


def _post_b_kernel(c_ref, z_ref, lw_ref, lb_ref, y_ref):
    c = c_ref[...]
    mu = jnp.mean(c, axis=-1, keepdims=True)
    cc = c - mu
    var = jnp.mean(cc * cc, axis=-1, keepdims=True)
    y = cc * lax.rsqrt(var + EPS) * lw_ref[...] + lb_ref[...]
    y_ref[...] = (_silu(y) * z_ref[...]).astype(y_ref.dtype)


def _post_b(c, u, lw, lb, *, layer, z_block, tm):
    S, W = c.shape
    return pl.pallas_call(
        _post_b_kernel,
        grid=(S // tm,),
        in_specs=[pl.BlockSpec((tm, W), lambda i: (i, 0)),
                  pl.BlockSpec((tm, W), lambda i: (i, z_block)),
                  pl.BlockSpec((None, 1, W), lambda i: (layer, 0, 0)),
                  pl.BlockSpec((None, 1, W), lambda i: (layer, 0, 0))],
        out_specs=pl.BlockSpec((tm, W), lambda i: (i, 0)),
        out_shape=jax.ShapeDtypeStruct((S, W), BF16),
        compiler_params=_params("arbitrary"),
        name="post_b",
    )(c, u, lw, lb)


def _merge_kernel(ya_ref, yb_ref, wa_ref, wb_ref, ga_ref, gb_ref, m_ref):
    pa = jnp.dot(ya_ref[...], wa_ref[...], preferred_element_type=F32)
    pb = jnp.dot(yb_ref[...], wb_ref[...], preferred_element_type=F32)
    m_ref[...] = (ga_ref[...] * pa + gb_ref[...] * pb).astype(m_ref.dtype)


def _merge(ya, yb, wa, wb, u, *, layer, ga_block, gb_block, tm, tn):
    S, W = ya.shape
    D = wa.shape[2]
    nj = D // tn
    return pl.pallas_call(
        _merge_kernel,
        grid=(S // tm, nj),
        in_specs=[pl.BlockSpec((tm, W), lambda i, j: (i, 0)),
                  pl.BlockSpec((tm, W), lambda i, j: (i, 0)),
                  pl.BlockSpec((None, W, tn), lambda i, j: (layer, 0, j)),
                  pl.BlockSpec((None, W, tn), lambda i, j: (layer, 0, j)),
                  pl.BlockSpec((tm, tn), lambda i, j: (i, ga_block * nj + j)),
                  pl.BlockSpec((tm, tn), lambda i, j: (i, gb_block * nj + j))],
        out_specs=pl.BlockSpec((tm, tn), lambda i, j: (i, j)),
        out_shape=jax.ShapeDtypeStruct((S, D), BF16),
        compiler_params=_params("arbitrary", "arbitrary"),
        name="merge",
    )(ya, yb, wa, wb, u, u)


def _outproj_kernel(m_ref, w_ref, x_ref, o_ref):
    o_ref[...] = x_ref[...] + jnp.dot(m_ref[...], w_ref[...], preferred_element_type=F32)


def _outproj(m, w, x, *, layer, tm, tn):
    S, D = x.shape
    return pl.pallas_call(
        _outproj_kernel,
        grid=(S // tm, D // tn),
        in_specs=[pl.BlockSpec((tm, D), lambda i, j: (i, 0)),
                  pl.BlockSpec((None, D, tn), lambda i, j: (layer, 0, j)),
                  pl.BlockSpec((tm, tn), lambda i, j: (i, j))],
        out_specs=pl.BlockSpec((tm, tn), lambda i, j: (i, j)),
        out_shape=jax.ShapeDtypeStruct((S, D), F32),
        compiler_params=_params("arbitrary", "arbitrary"),
        name="outproj",
    )(m, w, x)


def _final_norm_kernel(x_ref, w_ref, o_ref):
    x = x_ref[...]
    o_ref[...] = (x * lax.rsqrt(jnp.mean(x * x, axis=-1, keepdims=True) + EPS)) * w_ref[...]


def _final_norm(x, w, *, tm):
    S, D = x.shape
    return pl.pallas_call(
        _final_norm_kernel,
        grid=(S // tm,),
        in_specs=[pl.BlockSpec((tm, D), lambda i: (i, 0)), pl.BlockSpec((1, D), lambda i: (0, 0))],
        out_specs=pl.BlockSpec((tm, D), lambda i: (i, 0)),
        out_shape=jax.ShapeDtypeStruct((S, D), F32),
        compiler_params=_params("arbitrary"),
        name="final_norm",
    )(x, w)


def _tile(n, pref):
    t = min(n, pref)
    assert n % t == 0, (n, t)
    return t


def kernel(x, norm_w, w_in, b_in, qk_conv_w, mlstm_norm_w, w_a, dw_w, dw_b, ln_w, ln_b, w_b, w_out, final_norm_w):
    B, S, D = x.shape
    depth = norm_w.shape[0]
    W = w_a.shape[1]
    n_in = w_in.shape[2]
    heads = (n_in - 10 * W) // 4
    dh = W // heads
    assert B == 1 and W == D and w_b.shape[1] == W and n_in == 10 * W + 4 * heads
    assert S % (CHUNK * CHUNKS_PER_STEP) == 0 and dh % LANES == 0 and 2 * heads <= LANES
    assert heads % HEADS_PER_STEP == 0
    g0 = 5 * W
    ng = 4 * heads

    tm = _tile(S, 1024)
    tn = _tile(W, 1024)
    cw = _tile(W, 256)
    tr = _tile(S, 256)

    V = 2
    Q, K_, O, ZA, GLA, GLB, ZB, GA, GB = range(9)

    w_main = _wprep(w_in, gate_start=g0, gate_width=ng, tk=_tile(D, 512), tn=_tile(W, 2048))
    b_main = jnp.concatenate([b_in[:, :g0], b_in[:, g0 + ng:]], axis=1)[:, None, :]
    icols = lambda a: jnp.concatenate([a[..., g0:g0 + heads], a[..., g0 + 2 * heads:g0 + 3 * heads]], axis=-1)
    fcols = lambda a: jnp.concatenate([a[..., g0 + heads:g0 + 2 * heads], a[..., g0 + 3 * heads:g0 + ng]], axis=-1)
    padl = lambda a: jnp.pad(a, [(0, 0)] * (a.ndim - 1) + [(0, LANES - 2 * heads)])
    w_g = jnp.concatenate([padl(icols(w_in)), padl(fcols(w_in))], axis=-1).astype(BF16)
    b_g = jnp.concatenate([padl(icols(b_in)), padl(fcols(b_in))], axis=-1)[:, None, :]
    wa16, wb16, wo16 = w_a.astype(BF16), w_b.astype(BF16), w_out.astype(BF16)
    row3 = lambda a: a[:, None, :]
    zero_b = jnp.zeros((depth, 1, W), F32)
    ident = lambda a: a

    xs = x.reshape(S, D)
    for l in range(depth):
        u, v, gates = _inproj(xs, row3(norm_w), w_main, b_main, w_g, b_g, layer=l, width=W, v_block=V,
                              sigmoid_blocks=(3, 6, 8, 9), silu_blocks=(4, 7), tm=tm, tn=tn)
        qc = _stripe_conv([u], [Q], qk_conv_w, zero_b, layer=l, w_off=0, ncols=W, cw=cw, pre=ident, post=_silu,
                          transpose_out=False, scale=1.0, out_dtype=BF16, name="q_conv")
        kt = _stripe_conv([u], [K_], qk_conv_w, zero_b, layer=l, w_off=1, ncols=W, cw=cw, pre=ident, post=_silu,
                          transpose_out=True, scale=dh ** -0.5, out_dtype=F32, name="k_conv")
        prep = _gateprep(gates, heads)
        hb = _mlstm(qc, kt, v, prep[5:], heads=heads, reverse=True)
        ya = _mlstm(qc, kt, v, prep[:5], heads=heads, reverse=False,
                    epilogue_args=(hb, u, O, ZA, row3(mlstm_norm_w), l))

        cconv = _stripe_conv([u, u], [GLA, GLB], dw_w, row3(dw_b), layer=l, w_off=0, ncols=W,
                             cw=_tile(W, LANES), pre=lambda a, b: a * b, post=ident,
                             transpose_out=False, scale=1.0, out_dtype=F32, name="glu_conv")
        yb = _post_b(cconv, u, row3(ln_w), row3(ln_b), layer=l, z_block=ZB, tm=tr)

        merged = _merge(ya, yb, wa16, wb16, u, layer=l, ga_block=GA, gb_block=GB, tm=tm, tn=_tile(D, 512))
        xs = _outproj(merged, wo16, xs, layer=l, tm=tm, tn=tn)

    return _final_norm(xs, final_norm_w[None, :], tm=tr).reshape(B, S, D)
```

```python
import functools

import jax
import jax.numpy as jnp
from jax import lax
from jax.experimental import pallas as pl
from jax.experimental.pallas import tpu as pltpu

F32 = jnp.float32
BF16 = jnp.bfloat16

EPS = 1e-6
LANES = 128
SUBLANES = 8
MXU_DIM = 256
CHUNK = MXU_DIM
VMEM_LIMIT_BYTES = 56 * 2**20
CONV_PAD = 16
CONV_ROWS = 128
HEADS_PER_STEP = 2
CHUNKS_PER_STEP = 2


def _params(*sem):
    return pltpu.CompilerParams(dimension_semantics=sem, vmem_limit_bytes=VMEM_LIMIT_BYTES)


def _sigmoid(x):
    return 0.5 * jnp.tanh(0.5 * x) + 0.5


def _silu(x):
    return x * _sigmoid(x)


def _dot_t(a, bt):
    return lax.dot_general(a, bt, (((1,), (1,)), ((), ())), preferred_element_type=F32)


def _wprep_kernel(a_ref, nxt_ref, g_ref, o_ref, og_ref, *, first_shifted, shift):
    c = pl.program_id(1)

    @pl.when(c == 0)
    def _():
        og_ref[...] = jnp.zeros(og_ref.shape, BF16)
        og_ref[0:shift, :] = g_ref[...].astype(BF16)

    @pl.when(c < first_shifted)
    def _():
        o_ref[...] = a_ref[...].astype(BF16)

    @pl.when(c >= first_shifted)
    def _():
        rows = a_ref.shape[0]
        o_ref[0:rows - shift, :] = a_ref[shift:, :].astype(BF16)
        o_ref[rows - shift:, :] = nxt_ref[...].astype(BF16)


def _wprep(wt, *, gate_start, gate_width, tn):
    depth, N, D = wt.shape
    gw = gate_width
    assert gate_start % tn == 0 and (N - gw) % tn == 0 and tn % gw == 0 and gate_start % gw == 0
    assert gw % (2 * SUBLANES) == 0 and gw <= LANES
    return pl.pallas_call(
        functools.partial(_wprep_kernel, first_shifted=gate_start // tn, shift=gw),
        grid=(depth, (N - gw) // tn),
        in_specs=[pl.BlockSpec((None, tn, D), lambda l, c: (l, c, 0)),
                  pl.BlockSpec((None, gw, D), lambda l, c: (l, (c + 1) * (tn // gw), 0)),
                  pl.BlockSpec((None, gw, D), lambda l, c: (l, gate_start // gw, 0))],
        out_specs=[pl.BlockSpec((None, tn, D), lambda l, c: (l, c, 0)),
                   pl.BlockSpec((None, LANES, D), lambda l, c: (l, 0, 0))],
        out_shape=[jax.ShapeDtypeStruct((depth, N - gw, D), BF16),
                   jax.ShapeDtypeStruct((depth, LANES, D), BF16)],
        compiler_params=_params("arbitrary", "arbitrary"),
        name="wprep",
    )(wt, wt, wt)


def _inproj_kernel(x_ref, nw_ref, w_ref, b_ref, wg_ref, bg_ref, u_ref, v_ref, g_ref, hn_ref, *,
                   row_step, col_step, tiles_per_block, v_block, sigmoid_blocks, silu_blocks):
    j = pl.program_id(1)
    blk = j // tiles_per_block

    @pl.when(j == 0)
    def _():
        for r in range(0, x_ref.shape[0], row_step):
            x = x_ref[r:r + row_step, :]
            ms = jnp.mean(x * x, axis=-1, keepdims=True)
            hn_ref[r:r + row_step, :] = ((x * lax.rsqrt(ms + EPS)) * nw_ref[...]).astype(BF16)
        g_ref[...] = _dot_t(hn_ref[...], wg_ref[...]) + bg_ref[...]

    def emit(out_ref, act):
        for c in range(0, w_ref.shape[0], col_step):
            y = _dot_t(hn_ref[...], w_ref[c:c + col_step, :]) + b_ref[:, c:c + col_step]
            out_ref[:, c:c + col_step] = act(y).astype(out_ref.dtype)

    any_of = lambda blocks: functools.reduce(jnp.logical_or, [blk == k for k in blocks])
    is_v, is_sig, is_silu = blk == v_block, any_of(sigmoid_blocks), any_of(silu_blocks)
    is_plain = jnp.logical_not(jnp.logical_or(is_v, jnp.logical_or(is_sig, is_silu)))
    pl.when(is_v)(lambda: emit(v_ref, lambda y: y))
    pl.when(is_plain)(lambda: emit(u_ref, lambda y: y))
    pl.when(is_sig)(lambda: emit(u_ref, _sigmoid))
    pl.when(is_silu)(lambda: emit(u_ref, _silu))


def _inproj(x, nw, w, b, wg, bg, *, layer, width, v_block, sigmoid_blocks, silu_blocks, tm, tn):
    S, D = x.shape
    N = w.shape[1]
    NG = wg.shape[1]
    p = width // tn
    v_lo, v_hi = v_block * p, (v_block + 1) * p

    def u_idx(i, j):
        return i, jnp.where(j < v_lo, j, jnp.where(j < v_hi, v_lo - 1, j - p))

    def v_idx(i, j):
        return i, jnp.clip(j - v_lo, 0, p - 1)

    return pl.pallas_call(
        functools.partial(_inproj_kernel, row_step=min(tm, 128), col_step=min(tn, MXU_DIM), tiles_per_block=p,
                          v_block=v_block, sigmoid_blocks=sigmoid_blocks, silu_blocks=silu_blocks),
        grid=(S // tm, N // tn),
        in_specs=[pl.BlockSpec((tm, D), lambda i, j: (i, 0)),
                  pl.BlockSpec((None, 1, D), lambda i, j: (layer, 0, 0)),
                  pl.BlockSpec((None, tn, D), lambda i, j: (layer, j, 0)),
                  pl.BlockSpec((None, 1, tn), lambda i, j: (layer, 0, j)),
                  pl.BlockSpec((None, NG, D), lambda i, j: (layer, 0, 0)),
                  pl.BlockSpec((None, 1, NG), lambda i, j: (layer, 0, 0))],
        out_specs=[pl.BlockSpec((tm, tn), u_idx),
                   pl.BlockSpec((tm, tn), v_idx),
                   pl.BlockSpec((tm, NG), lambda i, j: (i, 0))],
        out_shape=[jax.ShapeDtypeStruct((S, N - width), F32),
                   jax.ShapeDtypeStruct((S, width), BF16),
                   jax.ShapeDtypeStruct((S, NG), F32)],
        scratch_shapes=[pltpu.VMEM((tm, D), BF16)],
        compiler_params=_params("arbitrary", "arbitrary"),
        name="inproj",
    )(x, nw, w, b, wg, bg)


def _stripe_conv_kernel(*refs, n_in, ktaps, pre, post, transpose_out, scale):
    in_refs = refs[:n_in]
    w_ref, b_ref, o_ref, pad_ref = refs[n_in:n_in + 4]
    S, cw = in_refs[0].shape
    P, R = CONV_PAD, CONV_ROWS
    win_rows = R + 2 * P
    half = ktaps // 2

    pad_ref[0:P, :] = jnp.zeros((P, cw), F32)
    pad_ref[P + S:P + S + P, :] = jnp.zeros((P, cw), F32)

    def fill(i, c):
        r0 = pl.multiple_of(i * R, R)
        vals = [r[pl.ds(r0, R), :] for r in in_refs]
        pad_ref[pl.ds(P + r0, R), :] = pre(*vals)
        return c

    lax.fori_loop(0, S // R, fill, 0)

    def body(i, c):
        r0 = pl.multiple_of(i * R, R)
        win = pad_ref[pl.ds(r0, win_rows), :]
        acc = jnp.zeros((R, cw), F32) + b_ref[...]
        for sub in range(SUBLANES):
            taps = [d for d in range(ktaps) if (P + d - half) % SUBLANES == sub]
            if not taps:
                continue
            wsub = win if sub == 0 else pltpu.roll(win, win_rows - sub, 0)
            for d in taps:
                a = (P + d - half) // SUBLANES
                acc = acc + w_ref[d:d + 1, :] * wsub[SUBLANES * a:SUBLANES * a + R, :]
        y = post(acc)
        if scale != 1.0:
            y = y * scale
        if transpose_out:
            o_ref[:, pl.ds(r0, R)] = y.T.astype(o_ref.dtype)
        else:
            o_ref[pl.ds(r0, R), :] = y.astype(o_ref.dtype)
        return c

    lax.fori_loop(0, S // R, body, 0)


def _stripe_conv(inputs, col_blocks, w, b, *, layer, w_off, ncols, cw, pre, post, transpose_out, scale,
                 out_dtype, name):
    S = inputs[0].shape[0]
    ktaps = w.shape[1]
    nj = ncols // cw
    in_specs = [pl.BlockSpec((S, cw), functools.partial(lambda j, off: (0, off + j), off=cb * nj))
                for cb in col_blocks]
    in_specs += [pl.BlockSpec((None, ktaps, cw), lambda j: (layer, 0, w_off * nj + j)),
                 pl.BlockSpec((None, 1, cw), lambda j: (layer, 0, j))]
    if transpose_out:
        out_spec = pl.BlockSpec((cw, S), lambda j: (j, 0))
        out_shape = jax.ShapeDtypeStruct((ncols, S), out_dtype)
    else:
        out_spec = pl.BlockSpec((S, cw), lambda j: (0, j))
        out_shape = jax.ShapeDtypeStruct((S, ncols), out_dtype)
    return pl.pallas_call(
        functools.partial(_stripe_conv_kernel, n_in=len(inputs), ktaps=ktaps, pre=pre, post=post,
                          transpose_out=transpose_out, scale=scale),
        grid=(nj,),
        in_specs=in_specs,
        out_specs=out_spec,
        out_shape=out_shape,
        scratch_shapes=[pltpu.VMEM((S + 2 * CONV_PAD, cw), F32)],
        compiler_params=_params("arbitrary"),
        name=name,
    )(*inputs, w, b)


def _split3(x):
    a = x.astype(BF16)
    r = x - a.astype(F32)
    b = r.astype(BF16)
    c = (r - b.astype(F32)).astype(BF16)
    return a, b, c


def _gateprep_kernel(gf_ref, gb_ref, *out_and_scratch, heads):
    outs = out_and_scratch[:10]
    m_ref = out_and_scratch[10]
    L = CHUNK

    @pl.when(pl.program_id(0) == 0)
    def _():
        m_ref[...] = jnp.zeros(m_ref.shape, F32)

    row = lax.broadcasted_iota(jnp.int32, (L, L), 0)
    col = lax.broadcasted_iota(jnp.int32, (L, L), 1)
    for d, g_ref in enumerate((gf_ref, gb_ref)):
        wi_ref, wr_ref, er_ref, wk_ref, dc_ref = outs[5 * d:5 * d + 5]
        keep = (col <= row) if d == 0 else (col >= row)
        tri = keep.astype(BF16)
        t = g_ref[...]
        ls = jnp.minimum(t, 0.0) - jnp.log1p(jnp.exp(-jnp.abs(t)))
        g = sum(jnp.dot(tri, p, preferred_element_type=F32) for p in _split3(ls))
        gT = g.T
        tT = t.T
        for h in range(heads):
            c = heads * d + h
            ci = 2 * heads * d + h
            cf = ci + heads
            m = m_ref[c:c + 1, :]
            g_col = g[:, cf:cf + 1]
            g_row = gT[cf:cf + 1, :]
            i_row = tT[ci:ci + 1, :]
            a = g_col + m
            dmat = jnp.where(keep, g_col - g_row + i_row, -jnp.inf)
            dmax = jnp.max(dmat, axis=-1, keepdims=True)
            m_rows = jnp.maximum(a, dmax)
            wi_ref[h] = jnp.exp(dmat - jnp.maximum(a[:, 0:1], dmax))
            wr_ref[h] = jnp.exp(a - m_rows)
            er_ref[h] = jnp.exp(-m_rows)
            gtot = g_row[:, L - 1:L] if d == 0 else g_row[:, 0:1]
            b_row = gtot - g_row + i_row
            m_new = jnp.maximum(gtot + m, jnp.max(b_row, axis=-1, keepdims=True))
            wk_ref[h] = jnp.broadcast_to(jnp.exp(b_row - m_new[:, 0:1]), (SUBLANES, L))
            dc_ref[h] = jnp.broadcast_to(jnp.exp(gtot + m - m_new), (SUBLANES, LANES))
            m_ref[c:c + 1, :] = m_new


def _gateprep(gates, heads):
    S = gates.shape[0]
    L = CHUNK
    nc = S // L
    fw = lambda j: (0, j, 0)
    bw = lambda j: (0, nc - 1 - j, 0)
    out_specs, out_shape = [], []
    for idx in (fw, bw):
        out_specs += [pl.BlockSpec((heads, L, L), idx), pl.BlockSpec((heads, L, LANES), idx),
                      pl.BlockSpec((heads, L, LANES), idx), pl.BlockSpec((heads, SUBLANES, L), idx),
                      pl.BlockSpec((heads, SUBLANES, LANES), idx)]
        out_shape += [jax.ShapeDtypeStruct((heads, S, L), F32), jax.ShapeDtypeStruct((heads, S, LANES), F32),
                      jax.ShapeDtypeStruct((heads, S, LANES), F32),
                      jax.ShapeDtypeStruct((heads, nc * SUBLANES, L), F32),
                      jax.ShapeDtypeStruct((heads, nc * SUBLANES, LANES), F32)]
    return pl.pallas_call(
        functools.partial(_gateprep_kernel, heads=heads),
        grid=(nc,),
        in_specs=[pl.BlockSpec((L, LANES), lambda j: (j, 0)),
                  pl.BlockSpec((L, LANES), lambda j: (nc - 1 - j, 0))],
        out_specs=out_specs,
        out_shape=out_shape,
        scratch_shapes=[pltpu.VMEM((2 * heads, LANES), F32)],
        compiler_params=_params("arbitrary"),
        name="gateprep",
    )(gates, gates)


def _mlstm_kernel(*refs, reverse, epilogue, dh):
    q_ref, kt_ref, v_ref, wi_ref, wr_ref, er_ref, wk_ref, dc_ref = refs[:8]
    if epilogue:
        hb_ref, o_ref, z_ref, nw_ref = refs[8:12]
        out_ref, c_ref, cb_ref, n_ref = refs[12:]
    else:
        out_ref, c_ref, cb_ref, n_ref = refs[8:]
    L = CHUNK
    nlb = dh // LANES

    @pl.when(pl.program_id(1) == 0)
    def _():
        c_ref[...] = jnp.zeros(c_ref.shape, F32)
        cb_ref[...] = jnp.zeros(cb_ref.shape, BF16)
        n_ref[...] = jnp.zeros(n_ref.shape, F32)

    ones = jnp.ones((L, LANES), BF16)
    chunks = range(CHUNKS_PER_STEP)
    for t in (reversed(chunks) if reverse else chunks):
        rows = slice(t * L, (t + 1) * L)
        srow = slice(t * SUBLANES, t * SUBLANES + 1)
        for hh in range(HEADS_PER_STEP):
            cols = slice(hh * dh, (hh + 1) * dh)
            q = q_ref[rows, cols]
            kt = kt_ref[cols, rows]
            v1 = jnp.concatenate([v_ref[rows, cols], ones], axis=1)
            nvec = n_ref[hh]
            rhs = jnp.concatenate([kt.astype(BF16), nvec.astype(BF16)], axis=1)
            qk2 = jnp.dot(q, rhs, preferred_element_type=F32)
            s = qk2[:, :L] * wi_ref[hh, rows, :]
            w_inter = wr_ref[hh, rows, :]
            sv = jnp.dot(s.astype(BF16), v1, preferred_element_type=F32)
            den = sv[:, dh:] + w_inter * qk2[:, L:]
            rden = 1.0 / jnp.maximum(jnp.abs(den), er_ref[hh, rows, :])
            qc = jnp.dot(q, cb_ref[hh], preferred_element_type=F32)
            hblk = [(sv[:, b * LANES:(b + 1) * LANES] + w_inter * qc[:, b * LANES:(b + 1) * LANES]) * rden
                    for b in range(nlb)]
            if epilogue:
                ht = []
                for b in range(nlb):
                    cb = slice(hh * dh + b * LANES, hh * dh + (b + 1) * LANES)
                    ht.append(o_ref[rows, cb] * (hblk[b] + hb_ref[rows, cb]))
                ssq = sum(jnp.sum(x * x, axis=-1, keepdims=True) for x in ht)
                rinv = lax.rsqrt(ssq * (1.0 / dh) + EPS)
                for b in range(nlb):
                    cb = slice(hh * dh + b * LANES, hh * dh + (b + 1) * LANES)
                    y = ((ht[b] * rinv) * nw_ref[:, cb]) * z_ref[rows, cb]
                    out_ref[rows, cb] = y.astype(out_ref.dtype)
            else:
                for b in range(nlb):
                    cb = slice(hh * dh + b * LANES, hh * dh + (b + 1) * LANES)
                    out_ref[rows, cb] = hblk[b]
            wkt = kt * wk_ref[hh, srow, :]
            dec = dc_ref[hh, srow, :]
            dcm = jnp.dot(wkt.astype(BF16), v1, preferred_element_type=F32)
            n_ref[hh] = dec * nvec + dcm[:, dh:]
            cnew = jnp.concatenate([dec] * nlb, axis=1) * c_ref[hh] + dcm[:, :dh]
            c_ref[hh] = cnew
            cb_ref[hh] = cnew.astype(BF16)


def _mlstm(qc, kt, v, prep, *, heads, reverse, epilogue_args=None):
    S, W = qc.shape
    dh = W // heads
    L = CHUNK
    T, HP = CHUNKS_PER_STEP, HEADS_PER_STEP
    ng = S // (T * L)
    wi, wr, er, wk, dc = prep
    g = (lambda j: ng - 1 - j) if reverse else (lambda j: j)
    rowblk = pl.BlockSpec((T * L, HP * dh), lambda h, j: (g(j), h))
    in_specs = [rowblk,
                pl.BlockSpec((HP * dh, T * L), lambda h, j: (h, g(j))),
                rowblk,
                pl.BlockSpec((HP, T * L, L), lambda h, j: (h, g(j), 0)),
                pl.BlockSpec((HP, T * L, LANES), lambda h, j: (h, g(j), 0)),
                pl.BlockSpec((HP, T * L, LANES), lambda h, j: (h, g(j), 0)),
                pl.BlockSpec((HP, T * SUBLANES, L), lambda h, j: (h, g(j), 0)),
                pl.BlockSpec((HP, T * SUBLANES, LANES), lambda h, j: (h, g(j), 0))]
    args = [qc, kt, v, wi, wr, er, wk, dc]
    if epilogue_args is not None:
        hb, u, o_block, z_block, nw, layer = epilogue_args
        nhp = heads // HP
        in_specs += [rowblk,
                     pl.BlockSpec((T * L, HP * dh), lambda h, j: (g(j), o_block * nhp + h)),
                     pl.BlockSpec((T * L, HP * dh), lambda h, j: (g(j), z_block * nhp + h)),
                     pl.BlockSpec((None, 1, HP * dh), lambda h, j: (layer, 0, h))]
        args += [hb, u, u, nw]
        out_dtype = BF16
    else:
        out_dtype = F32
    return pl.pallas_call(
        functools.partial(_mlstm_kernel, reverse=reverse, epilogue=epilogue_args is not None, dh=dh),
        grid=(heads // HP, ng),
        in_specs=in_specs,
        out_specs=rowblk,
        out_shape=jax.ShapeDtypeStruct((S, W), out_dtype),
        scratch_shapes=[pltpu.VMEM((HP, dh, dh), F32), pltpu.VMEM((HP, dh, dh), BF16),
                        pltpu.VMEM((HP, dh, LANES), F32)],
        compiler_params=_params("arbitrary", "arbitrary"),
        name="mlstm_fw" if epilogue_args is not None else "mlstm_bw",
    )(*args)


def _post_b_kernel(c_ref, z_ref, lw_ref, lb_ref, y_ref):
    c = c_ref[...]
    mu = jnp.mean(c, axis=-1, keepdims=True)
    cc = c - mu
    var = jnp.mean(cc * cc, axis=-1, keepdims=True)
    y = cc * lax.rsqrt(var + EPS) * lw_ref[...] + lb_ref[...]
    y_ref[...] = (_silu(y) * z_ref[...]).astype(y_ref.dtype)


def _post_b(c, u, lw, lb, *, layer, z_block, tm):
    S, W = c.shape
    return pl.pallas_call(
        _post_b_kernel,
        grid=(S // tm,),
        in_specs=[pl.BlockSpec((tm, W), lambda i: (i, 0)),
                  pl.BlockSpec((tm, W), lambda i: (i, z_block)),
                  pl.BlockSpec((None, 1, W), lambda i: (layer, 0, 0)),
                  pl.BlockSpec((None, 1, W), lambda i: (layer, 0, 0))],
        out_specs=pl.BlockSpec((tm, W), lambda i: (i, 0)),
        out_shape=jax.ShapeDtypeStruct((S, W), BF16),
        compiler_params=_params("arbitrary"),
        name="post_b",
    )(c, u, lw, lb)


def _merge_kernel(ya_ref, yb_ref, wa_ref, wb_ref, ga_ref, gb_ref, m_ref):
    pa = jnp.dot(ya_ref[...], wa_ref[...], preferred_element_type=F32)
    pb = jnp.dot(yb_ref[...], wb_ref[...], preferred_element_type=F32)
    m_ref[...] = (ga_ref[...] * pa + gb_ref[...] * pb).astype(m_ref.dtype)


def _merge(ya, yb, wa, wb, u, *, layer, ga_block, gb_block, tm, tn):
    S, W = ya.shape
    D = wa.shape[2]
    nj = D // tn
    return pl.pallas_call(
        _merge_kernel,
        grid=(S // tm, nj),
        in_specs=[pl.BlockSpec((tm, W), lambda i, j: (i, 0)),
                  pl.BlockSpec((tm, W), lambda i, j: (i, 0)),
                  pl.BlockSpec((None, W, tn), lambda i, j: (layer, 0, j)),
                  pl.BlockSpec((None, W, tn), lambda i, j: (layer, 0, j)),
                  pl.BlockSpec((tm, tn), lambda i, j: (i, ga_block * nj + j)),
                  pl.BlockSpec((tm, tn), lambda i, j: (i, gb_block * nj + j))],
        out_specs=pl.BlockSpec((tm, tn), lambda i, j: (i, j)),
        out_shape=jax.ShapeDtypeStruct((S, D), BF16),
        compiler_params=_params("arbitrary", "arbitrary"),
        name="merge",
    )(ya, yb, wa, wb, u, u)


def _outproj_kernel(m_ref, w_ref, x_ref, o_ref):
    o_ref[...] = x_ref[...] + jnp.dot(m_ref[...], w_ref[...], preferred_element_type=F32)


def _outproj(m, w, x, *, layer, tm, tn):
    S, D = x.shape
    return pl.pallas_call(
        _outproj_kernel,
        grid=(S // tm, D // tn),
        in_specs=[pl.BlockSpec((tm, D), lambda i, j: (i, 0)),
                  pl.BlockSpec((None, D, tn), lambda i, j: (layer, 0, j)),
                  pl.BlockSpec((tm, tn), lambda i, j: (i, j))],
        out_specs=pl.BlockSpec((tm, tn), lambda i, j: (i, j)),
        out_shape=jax.ShapeDtypeStruct((S, D), F32),
        compiler_params=_params("arbitrary", "arbitrary"),
        name="outproj",
    )(m, w, x)


def _final_norm_kernel(x_ref, w_ref, o_ref):
    x = x_ref[...]
    o_ref[...] = (x * lax.rsqrt(jnp.mean(x * x, axis=-1, keepdims=True) + EPS)) * w_ref[...]


def _final_norm(x, w, *, tm):
    S, D = x.shape
    return pl.pallas_call(
        _final_norm_kernel,
        grid=(S // tm,),
        in_specs=[pl.BlockSpec((tm, D), lambda i: (i, 0)), pl.BlockSpec((1, D), lambda i: (0, 0))],
        out_specs=pl.BlockSpec((tm, D), lambda i: (i, 0)),
        out_shape=jax.ShapeDtypeStruct((S, D), F32),
        compiler_params=_params("arbitrary"),
        name="final_norm",
    )(x, w)


def _tile(n, pref):
    t = min(n, pref)
    assert n % t == 0, (n, t)
    return t


def kernel(x, norm_w, w_in, b_in, qk_conv_w, mlstm_norm_w, w_a, dw_w, dw_b, ln_w, ln_b, w_b, w_out, final_norm_w):
    B, S, D = x.shape
    depth = norm_w.shape[0]
    W = w_a.shape[1]
    n_in = w_in.shape[2]
    heads = (n_in - 10 * W) // 4
    dh = W // heads
    assert B == 1 and W == D and w_b.shape[1] == W and n_in == 10 * W + 4 * heads
    assert S % (CHUNK * CHUNKS_PER_STEP) == 0 and dh % LANES == 0 and 2 * heads <= LANES
    assert heads % HEADS_PER_STEP == 0
    g0 = 5 * W
    ng = 4 * heads

    tm = _tile(S, 1024)
    tn = _tile(W, 1024)
    cw = _tile(W, 256)
    tr = _tile(S, 256)

    V = 2
    Q, K_, O, ZA, GLA, GLB, ZB, GA, GB = range(9)

    w_main, w_g = _wprep(jnp.swapaxes(w_in, 1, 2), gate_start=g0, gate_width=ng, tn=_tile(W, 1024))
    b_main = jnp.concatenate([b_in[:, :g0], b_in[:, g0 + ng:]], axis=1)[:, None, :]
    b_g = jnp.pad(b_in[:, g0:g0 + ng], [(0, 0), (0, LANES - ng)])[:, None, :]
    wa16, wb16, wo16 = w_a.astype(BF16), w_b.astype(BF16), w_out.astype(BF16)
    row3 = lambda a: a[:, None, :]
    zero_b = jnp.zeros((depth, 1, W), F32)
    ident = lambda a: a

    xs = x.reshape(S, D)
    for l in range(depth):
        u, v, gates = _inproj(xs, row3(norm_w), w_main, b_main, w_g, b_g, layer=l, width=W, v_block=V,
                              sigmoid_blocks=(3, 6, 8, 9), silu_blocks=(4, 7), tm=tm, tn=tn)
        qc = _stripe_conv([u], [Q], qk_conv_w, zero_b, layer=l, w_off=0, ncols=W, cw=cw, pre=ident, post=_silu,
                          transpose_out=False, scale=1.0, out_dtype=BF16, name="q_conv")
        kt = _stripe_conv([u], [K_], qk_conv_w, zero_b, layer=l, w_off=1, ncols=W, cw=cw, pre=ident, post=_silu,
                          transpose_out=True, scale=dh ** -0.5, out_dtype=F32, name="k_conv")
        prep = _gateprep(gates, heads)
        hb = _mlstm(qc, kt, v, prep[5:], heads=heads, reverse=True)
        ya = _mlstm(qc, kt, v, prep[:5], heads=heads, reverse=False,
                    epilogue_args=(hb, u, O, ZA, row3(mlstm_norm_w), l))

        cconv = _stripe_conv([u, u], [GLA, GLB], dw_w, row3(dw_b), layer=l, w_off=0, ncols=W,
                             cw=_tile(W, LANES), pre=lambda a, b: a * b, post=ident,
                             transpose_out=False, scale=1.0, out_dtype=F32, name="glu_conv")
        yb = _post_b(cconv, u, row3(ln_w), row3(ln_b), layer=l, z_block=ZB, tm=tr)

        merged = _merge(ya, yb, wa16, wb16, u, layer=l, ga_block=GA, gb_block=GB, tm=tm, tn=_tile(D, 512))
        xs = _outproj(merged, wo16, xs, layer=l, tm=tm, tn=tn)

    return _final_norm(xs, final_norm_w[None, :], tm=tr).reshape(B, S, D)
```

```python
import functools

import jax
import jax.numpy as jnp
from jax import lax
from jax.experimental import pallas as pl
from jax.experimental.pallas import tpu as pltpu

F32 = jnp.float32
BF16 = jnp.bfloat16

EPS = 1e-6
LANES = 128
SUBLANES = 8
MXU_DIM = 256
CHUNK = MXU_DIM
VMEM_LIMIT_BYTES = 56 * 2**20
CONV_PAD = 16
CONV_ROWS = 128
CONV_PIECE_ROWS = 64
HEADS_PER_STEP = 2
CHUNKS_PER_STEP = 2


def _params(*sem):
    return pltpu.CompilerParams(dimension_semantics=sem, vmem_limit_bytes=VMEM_LIMIT_BYTES)


def _sigmoid(x):
    return 0.5 * jnp.tanh(0.5 * x) + 0.5


def _silu(x):
    return x * _sigmoid(x)


def _dot_t(a, bt):
    return lax.dot_general(a, bt, (((1,), (1,)), ((), ())), preferred_element_type=F32)


def _wprep_kernel(a_ref, nxt_ref, g_ref, o_ref, og_ref, *, first_shifted, shift):
    c = pl.program_id(1)

    @pl.when(c == 0)
    def _():
        og_ref[...] = jnp.zeros(og_ref.shape, BF16)
        og_ref[0:shift, :] = g_ref[...].astype(BF16)

    @pl.when(c < first_shifted)
    def _():
        o_ref[...] = a_ref[...].astype(BF16)

    @pl.when(c >= first_shifted)
    def _():
        rows = a_ref.shape[0]
        o_ref[0:rows - shift, :] = a_ref[shift:, :].astype(BF16)
        o_ref[rows - shift:, :] = nxt_ref[...].astype(BF16)


def _wprep(wt, *, gate_start, gate_width, tn):
    depth, N, D = wt.shape
    gw = gate_width
    assert gate_start % tn == 0 and (N - gw) % tn == 0 and tn % gw == 0 and gate_start % gw == 0
    assert gw % (2 * SUBLANES) == 0 and gw <= LANES
    return pl.pallas_call(
        functools.partial(_wprep_kernel, first_shifted=gate_start // tn, shift=gw),
        grid=(depth, (N - gw) // tn),
        in_specs=[pl.BlockSpec((None, tn, D), lambda l, c: (l, c, 0)),
                  pl.BlockSpec((None, gw, D), lambda l, c: (l, (c + 1) * (tn // gw), 0)),
                  pl.BlockSpec((None, gw, D), lambda l, c: (l, gate_start // gw, 0))],
        out_specs=[pl.BlockSpec((None, tn, D), lambda l, c: (l, c, 0)),
                   pl.BlockSpec((None, LANES, D), lambda l, c: (l, 0, 0))],
        out_shape=[jax.ShapeDtypeStruct((depth, N - gw, D), BF16),
                   jax.ShapeDtypeStruct((depth, LANES, D), BF16)],
        compiler_params=_params("arbitrary", "arbitrary"),
        name="wprep",
    )(wt, wt, wt)


def _inproj_kernel(x_ref, nw_ref, w_ref, b_ref, wg_ref, bg_ref, u_ref, v_ref, g_ref, hn_ref, *,
                   row_step, col_step, tiles_per_block, kinds):
    j = pl.program_id(1)
    blk = j // tiles_per_block

    @pl.when(j == 0)
    def _():
        for r in range(0, x_ref.shape[0], row_step):
            x = x_ref[r:r + row_step, :]
            ms = jnp.mean(x * x, axis=-1, keepdims=True)
            hn_ref[r:r + row_step, :] = ((x * lax.rsqrt(ms + EPS)) * nw_ref[...]).astype(BF16)
        g_ref[...] = _dot_t(hn_ref[...], wg_ref[...]) + bg_ref[...]

    def emit(out_ref, act):
        for c in range(0, w_ref.shape[0], col_step):
            y = _dot_t(hn_ref[...], w_ref[c:c + col_step, :]) + b_ref[:, c:c + col_step]
            out_ref[:, c:c + col_step] = act(y).astype(out_ref.dtype)

    any_of = lambda kind: functools.reduce(jnp.logical_or, [blk == k for k, kd in enumerate(kinds) if kd == kind])
    pl.when(any_of("values"))(lambda: emit(v_ref, lambda y: y))
    pl.when(any_of("sigmoid"))(lambda: emit(u_ref, _sigmoid))
    pl.when(any_of("silu"))(lambda: emit(u_ref, _silu))


def _inproj(x, nw, w, b, wg, bg, *, layer, width, blocks, kinds, tm, tn):
    S, D = x.shape
    NG = wg.shape[1]
    p = width // tn
    assert kinds[0] == "values" and "values" not in kinds[1:]

    def src(j):
        lb = j // p
        blk = functools.reduce(lambda acc, kb: jnp.where(lb == kb[0], kb[1], acc), list(enumerate(blocks)), 0)
        return blk * p + j % p

    return pl.pallas_call(
        functools.partial(_inproj_kernel, row_step=min(tm, 128), col_step=min(tn, MXU_DIM), tiles_per_block=p,
                          kinds=kinds),
        grid=(S // tm, len(blocks) * p),
        in_specs=[pl.BlockSpec((tm, D), lambda i, j: (i, 0)),
                  pl.BlockSpec((None, 1, D), lambda i, j: (layer, 0, 0)),
                  pl.BlockSpec((None, tn, D), lambda i, j: (layer, src(j), 0)),
                  pl.BlockSpec((None, 1, tn), lambda i, j: (layer, 0, src(j))),
                  pl.BlockSpec((None, NG, D), lambda i, j: (layer, 0, 0)),
                  pl.BlockSpec((None, 1, NG), lambda i, j: (layer, 0, 0))],
        out_specs=[pl.BlockSpec((tm, tn), lambda i, j: (i, jnp.maximum(j - p, 0))),
                   pl.BlockSpec((tm, tn), lambda i, j: (i, jnp.minimum(j, p - 1))),
                   pl.BlockSpec((tm, NG), lambda i, j: (i, 0)),
                   pl.BlockSpec((tm, D), lambda i, j: (i, 0))],
        out_shape=[jax.ShapeDtypeStruct((S, (len(blocks) - 1) * width), F32),
                   jax.ShapeDtypeStruct((S, width), BF16),
                   jax.ShapeDtypeStruct((S, NG), F32),
                   jax.ShapeDtypeStruct((S, D), BF16)],
        compiler_params=_params("arbitrary", "arbitrary"),
        name="inproj",
    )(x, nw, w, b, wg, bg)


def _proj_conv_kernel(*refs, n_mats, ktaps, post, scale, transpose_out, n_row_blocks, cwid):
    hn_ref = refs[0]
    w_refs = refs[1:1 + n_mats]
    b_refs = refs[1 + n_mats:1 + 2 * n_mats]
    cw_ref, cb_ref, o_ref, buf_ref, y_ref = refs[1 + 2 * n_mats:]
    i = pl.program_id(1)
    tm, tn = y_ref.shape
    P, R, RP = CONV_PAD, CONV_ROWS, CONV_PIECE_ROWS
    win_rows = RP + 2 * P
    half = ktaps // 2
    nrb = tm // R

    col_pieces = [slice(c, c + MXU_DIM) for c in range(0, tn, MXU_DIM)] if tn > MXU_DIM else [slice(0, tn)]

    def project(cs):
        ys = [_dot_t(hn_ref[...], w[cs, :]) + b[:, cs] for w, b in zip(w_refs, b_refs)]
        y_ref[:, cs] = ys[0] if n_mats == 1 else ys[0] * _sigmoid(ys[1])

    def conv_piece(r0, cs):
        win = buf_ref[r0:r0 + win_rows, cs]
        acc = jnp.zeros((RP, cwid), F32) + cb_ref[:, cs]
        for sub in range(SUBLANES):
            taps = [d for d in range(ktaps) if (P + d - half) % SUBLANES == sub]
            if not taps:
                continue
            wsub = win if sub == 0 else pltpu.roll(win, win_rows - sub, 0)
            for d in taps:
                a = (P + d - half) // SUBLANES
                acc = acc + cw_ref[d:d + 1, cs] * wsub[SUBLANES * a:SUBLANES * a + RP, :]
        y = post(acc)
        return y if scale == 1.0 else y * scale

    def conv_rows(rb):
        for c0 in range(0, tn, cwid):
            cs = slice(c0, c0 + cwid)
            y = jnp.concatenate([conv_piece(rb * R + r, cs) for r in range(0, R, RP)], axis=0)
            if transpose_out:
                o_ref[cs, rb * R:(rb + 1) * R] = y.T.astype(o_ref.dtype)
            else:
                o_ref[rb * R:(rb + 1) * R, cs] = y.astype(o_ref.dtype)

    @pl.when(i == 0)
    def _():
        for cs in col_pieces:
            project(cs)
        buf_ref[0:P, :] = jnp.zeros((P, tn), F32)
        buf_ref[P:P + tm, :] = y_ref[...]

    @pl.when(jnp.logical_and(i > 0, i < n_row_blocks))
    def _():
        npc = len(col_pieces)
        for k, cs in enumerate(col_pieces):
            for rb in range(k * (nrb - 1) // npc, (k + 1) * (nrb - 1) // npc):
                conv_rows(rb)
            project(cs)
        buf_ref[P + tm:, :] = y_ref[0:P, :]
        conv_rows(nrb - 1)
        buf_ref[0:P, :] = buf_ref[tm:tm + P, :]
        buf_ref[P:P + tm, :] = y_ref[...]

    @pl.when(i == n_row_blocks)
    def _():
        buf_ref[P + tm:, :] = jnp.zeros((P, tn), F32)
        for rb in range(nrb):
            conv_rows(rb)


def _proj_conv(hn, w, b, cw, cb, *, layer, width, blocks, w_off, post, scale, transpose_out, out_dtype,
               tm, tn, cwid, name):
    S, D = hn.shape
    ktaps = cw.shape[1]
    p = width // tn
    ni = S // tm
    row = lambda i: jnp.maximum(i - 1, 0)
    in_specs = [pl.BlockSpec((tm, D), lambda j, i: (jnp.minimum(i, ni - 1), 0))]
    in_specs += [pl.BlockSpec((None, tn, D), functools.partial(lambda j, i, o: (layer, o + j, 0), o=blk * p))
                 for blk in blocks]
    in_specs += [pl.BlockSpec((None, 1, tn), functools.partial(lambda j, i, o: (layer, 0, o + j), o=blk * p))
                 for blk in blocks]
    in_specs += [pl.BlockSpec((None, ktaps, tn), lambda j, i: (layer, 0, w_off * p + j)),
                 pl.BlockSpec((None, 1, tn), lambda j, i: (layer, 0, j))]
    if transpose_out:
        out_spec = pl.BlockSpec((tn, tm), lambda j, i: (j, row(i)))
        out_shape = jax.ShapeDtypeStruct((width, S), out_dtype)
    else:
        out_spec = pl.BlockSpec((tm, tn), lambda j, i: (row(i), j))
        out_shape = jax.ShapeDtypeStruct((S, width), out_dtype)
    return pl.pallas_call(
        functools.partial(_proj_conv_kernel, n_mats=len(blocks), ktaps=ktaps, post=post, scale=scale,
                          transpose_out=transpose_out, n_row_blocks=ni, cwid=cwid),
        grid=(p, ni + 1),
        in_specs=in_specs,
        out_specs=out_spec,
        out_shape=out_shape,
        scratch_shapes=[pltpu.VMEM((tm + 2 * CONV_PAD, tn), F32), pltpu.VMEM((tm, tn), F32)],
        compiler_params=_params("arbitrary", "arbitrary"),
        name=name,
    )(hn, *([w] * len(blocks)), *([b] * len(blocks)), cw, cb)


def _split3(x):
    a = x.astype(BF16)
    r = x - a.astype(F32)
    b = r.astype(BF16)
    c = (r - b.astype(F32)).astype(BF16)
    return a, b, c


def _gateprep_kernel(gf_ref, gb_ref, *out_and_scratch, heads):
    outs = out_and_scratch[:10]
    m_ref = out_and_scratch[10]
    L = CHUNK

    @pl.when(pl.program_id(0) == 0)
    def _():
        m_ref[...] = jnp.zeros(m_ref.shape, F32)

    row = lax.broadcasted_iota(jnp.int32, (L, L), 0)
    col = lax.broadcasted_iota(jnp.int32, (L, L), 1)
    for d, g_ref in enumerate((gf_ref, gb_ref)):
        wi_ref, wr_ref, er_ref, wk_ref, dc_ref = outs[5 * d:5 * d + 5]
        keep = (col <= row) if d == 0 else (col >= row)
        tri = keep.astype(BF16)
        t = g_ref[...]
        ls = jnp.minimum(t, 0.0) - jnp.log1p(jnp.exp(-jnp.abs(t)))
        g = sum(jnp.dot(tri, p, preferred_element_type=F32) for p in _split3(ls))
        gT = g.T
        tT = t.T
        for h in range(heads):
            c = heads * d + h
            ci = 2 * heads * d + h
            cf = ci + heads
            m = m_ref[c:c + 1, :]
            g_col = g[:, cf:cf + 1]
            g_row = gT[cf:cf + 1, :]
            i_row = tT[ci:ci + 1, :]
            a = g_col + m
            dmat = jnp.where(keep, g_col - g_row + i_row, -jnp.inf)
            dmax = jnp.max(dmat, axis=-1, keepdims=True)
            m_rows = jnp.maximum(a, dmax)
            wi_ref[h] = jnp.exp(dmat - jnp.maximum(a[:, 0:1], dmax))
            wr_ref[h] = jnp.exp(a - m_rows)
            er_ref[h] = jnp.exp(-m_rows)
            gtot = g_row[:, L - 1:L] if d == 0 else g_row[:, 0:1]
            b_row = gtot - g_row + i_row
            m_new = jnp.maximum(gtot + m, jnp.max(b_row, axis=-1, keepdims=True))
            wk_ref[h] = jnp.broadcast_to(jnp.exp(b_row - m_new[:, 0:1]), (SUBLANES, L))
            dc_ref[h] = jnp.broadcast_to(jnp.exp(gtot + m - m_new), (SUBLANES, LANES))
            m_ref[c:c + 1, :] = m_new


def _gateprep(gates, heads):
    S = gates.shape[0]
    L = CHUNK
    nc = S // L
    fw = lambda j: (0, j, 0)
    bw = lambda j: (0, nc - 1 - j, 0)
    out_specs, out_shape = [], []
    for idx in (fw, bw):
        out_specs += [pl.BlockSpec((heads, L, L), idx), pl.BlockSpec((heads, L, LANES), idx),
                      pl.BlockSpec((heads, L, LANES), idx), pl.BlockSpec((heads, SUBLANES, L), idx),
                      pl.BlockSpec((heads, SUBLANES, LANES), idx)]
        out_shape += [jax.ShapeDtypeStruct((heads, S, L), F32), jax.ShapeDtypeStruct((heads, S, LANES), F32),
                      jax.ShapeDtypeStruct((heads, S, LANES), F32),
                      jax.ShapeDtypeStruct((heads, nc * SUBLANES, L), F32),
                      jax.ShapeDtypeStruct((heads, nc * SUBLANES, LANES), F32)]
    return pl.pallas_call(
        functools.partial(_gateprep_kernel, heads=heads),
        grid=(nc,),
        in_specs=[pl.BlockSpec((L, LANES), lambda j: (j, 0)),
                  pl.BlockSpec((L, LANES), lambda j: (nc - 1 - j, 0))],
        out_specs=out_specs,
        out_shape=out_shape,
        scratch_shapes=[pltpu.VMEM((2 * heads, LANES), F32)],
        compiler_params=_params("arbitrary"),
        name="gateprep",
    )(gates, gates)


def _mlstm_kernel(*refs, reverse, epilogue, dh):
    q_ref, kt_ref, v_ref, wi_ref, wr_ref, er_ref, wk_ref, dc_ref = refs[:8]
    if epilogue:
        hb_ref, o_ref, z_ref, nw_ref = refs[8:12]
        out_ref, c_ref, cb_ref, n_ref = refs[12:]
    else:
        out_ref, c_ref, cb_ref, n_ref = refs[8:]
    L = CHUNK
    nlb = dh // LANES

    @pl.when(pl.program_id(1) == 0)
    def _():
        c_ref[...] = jnp.zeros(c_ref.shape, F32)
        cb_ref[...] = jnp.zeros(cb_ref.shape, BF16)
        n_ref[...] = jnp.zeros(n_ref.shape, F32)

    ones = jnp.ones((L, LANES), BF16)
    chunks = range(CHUNKS_PER_STEP)
    for t in (reversed(chunks) if reverse else chunks):
        rows = slice(t * L, (t + 1) * L)
        srow = slice(t * SUBLANES, t * SUBLANES + 1)
        for hh in range(HEADS_PER_STEP):
            cols = slice(hh * dh, (hh + 1) * dh)
            q = q_ref[rows, cols]
            kt = kt_ref[cols, rows]
            v1 = jnp.concatenate([v_ref[rows, cols], ones], axis=1)
            nvec = n_ref[hh]
            rhs = jnp.concatenate([kt.astype(BF16), nvec.astype(BF16)], axis=1)
            qk2 = jnp.dot(q, rhs, preferred_element_type=F32)
            s = qk2[:, :L] * wi_ref[hh, rows, :]
            w_inter = wr_ref[hh, rows, :]
            sv = jnp.dot(s.astype(BF16), v1, preferred_element_type=F32)
            den = sv[:, dh:] + w_inter * qk2[:, L:]
            rden = 1.0 / jnp.maximum(jnp.abs(den), er_ref[hh, rows, :])
            qc = jnp.dot(q, cb_ref[hh], preferred_element_type=F32)
            hblk = [(sv[:, b * LANES:(b + 1) * LANES] + w_inter * qc[:, b * LANES:(b + 1) * LANES]) * rden
                    for b in range(nlb)]
            if epilogue:
                ht = []
                for b in range(nlb):
                    cb = slice(hh * dh + b * LANES, hh * dh + (b + 1) * LANES)
                    ht.append(o_ref[rows, cb] * (hblk[b] + hb_ref[rows, cb]))
                ssq = sum(jnp.sum(x * x, axis=-1, keepdims=True) for x in ht)
                rinv = lax.rsqrt(ssq * (1.0 / dh) + EPS)
                for b in range(nlb):
                    cb = slice(hh * dh + b * LANES, hh * dh + (b + 1) * LANES)
                    y = ((ht[b] * rinv) * nw_ref[:, cb]) * z_ref[rows, cb]
                    out_ref[rows, cb] = y.astype(out_ref.dtype)
            else:
                for b in range(nlb):
                    cb = slice(hh * dh + b * LANES, hh * dh + (b + 1) * LANES)
                    out_ref[rows, cb] = hblk[b]
            wkt = kt * wk_ref[hh, srow, :]
            dec = dc_ref[hh, srow, :]
            dcm = jnp.dot(wkt.astype(BF16), v1, preferred_element_type=F32)
            n_ref[hh] = dec * nvec + dcm[:, dh:]
            cnew = jnp.concatenate([dec] * nlb, axis=1) * c_ref[hh] + dcm[:, :dh]
            c_ref[hh] = cnew
            cb_ref[hh] = cnew.astype(BF16)


def _mlstm(qc, kt, v, prep, *, heads, reverse, epilogue_args=None):
    S, W = qc.shape
    dh = W // heads
    L = CHUNK
    T, HP = CHUNKS_PER_STEP, HEADS_PER_STEP
    ng = S // (T * L)
    wi, wr, er, wk, dc = prep
    g = (lambda j: ng - 1 - j) if reverse else (lambda j: j)
    rowblk = pl.BlockSpec((T * L, HP * dh), lambda h, j: (g(j), h))
    in_specs = [rowblk,
                pl.BlockSpec((HP * dh, T * L), lambda h, j: (h, g(j))),
                rowblk,
                pl.BlockSpec((HP, T * L, L), lambda h, j: (h, g(j), 0)),
                pl.BlockSpec((HP, T * L, LANES), lambda h, j: (h, g(j), 0)),
                pl.BlockSpec((HP, T * L, LANES), lambda h, j: (h, g(j), 0)),
                pl.BlockSpec((HP, T * SUBLANES, L), lambda h, j: (h, g(j), 0)),
                pl.BlockSpec((HP, T * SUBLANES, LANES), lambda h, j: (h, g(j), 0))]
    args = [qc, kt, v, wi, wr, er, wk, dc]
    if epilogue_args is not None:
        hb, u, o_block, z_block, nw, layer = epilogue_args
        nhp = heads // HP
        in_specs += [rowblk,
                     pl.BlockSpec((T * L, HP * dh), lambda h, j: (g(j), o_block * nhp + h)),
                     pl.BlockSpec((T * L, HP * dh), lambda h, j: (g(j), z_block * nhp + h)),
                     pl.BlockSpec((None, 1, HP * dh), lambda h, j: (layer, 0, h))]
        args += [hb, u, u, nw]
        out_dtype = BF16
    else:
        out_dtype = F32
    return pl.pallas_call(
        functools.partial(_mlstm_kernel, reverse=reverse, epilogue=epilogue_args is not None, dh=dh),
        grid=(heads // HP, ng),
        in_specs=in_specs,
        out_specs=rowblk,
        out_shape=jax.ShapeDtypeStruct((S, W), out_dtype),
        scratch_shapes=[pltpu.VMEM((HP, dh, dh), F32), pltpu.VMEM((HP, dh, dh), BF16),
                        pltpu.VMEM((HP, dh, LANES), F32)],
        compiler_params=_params("arbitrary", "arbitrary"),
        name="mlstm_fw" if epilogue_args is not None else "mlstm_bw",
    )(*args)


def _post_b_kernel(c_ref, z_ref, lw_ref, lb_ref, y_ref):
    c = c_ref[...]
    mu = jnp.mean(c, axis=-1, keepdims=True)
    cc = c - mu
    var = jnp.mean(cc * cc, axis=-1, keepdims=True)
    y = cc * lax.rsqrt(var + EPS) * lw_ref[...] + lb_ref[...]
    y_ref[...] = (_silu(y) * z_ref[...]).astype(y_ref.dtype)


def _post_b(c, u, lw, lb, *, layer, z_block, tm):
    S, W = c.shape
    return pl.pallas_call(
        _post_b_kernel,
        grid=(S // tm,),
        in_specs=[pl.BlockSpec((tm, W), lambda i: (i, 0)),
                  pl.BlockSpec((tm, W), lambda i: (i, z_block)),
                  pl.BlockSpec((None, 1, W), lambda i: (layer, 0, 0)),
                  pl.BlockSpec((None, 1, W), lambda i: (layer, 0, 0))],
        out_specs=pl.BlockSpec((tm, W), lambda i: (i, 0)),
        out_shape=jax.ShapeDtypeStruct((S, W), BF16),
        compiler_params=_params("arbitrary"),
        name="post_b",
    )(c, u, lw, lb)


def _merge_kernel(ya_ref, yb_ref, wa_ref, wb_ref, ga_ref, gb_ref, m_ref):
    pa = jnp.dot(ya_ref[...], wa_ref[...], preferred_element_type=F32)
    pb = jnp.dot(yb_ref[...], wb_ref[...], preferred_element_type=F32)
    m_ref[...] = (ga_ref[...] * pa + gb_ref[...] * pb).astype(m_ref.dtype)


def _merge(ya, yb, wa, wb, u, *, layer, ga_block, gb_block, tm, tn):
    S, W = ya.shape
    D = wa.shape[2]
    nj = D // tn
    return pl.pallas_call(
        _merge_kernel,
        grid=(S // tm, nj),
        in_specs=[pl.BlockSpec((tm, W), lambda i, j: (i, 0)),
                  pl.BlockSpec((tm, W), lambda i, j: (i, 0)),
                  pl.BlockSpec((None, W, tn), lambda i, j: (layer, 0, j)),
                  pl.BlockSpec((None, W, tn), lambda i, j: (layer, 0, j)),
                  pl.BlockSpec((tm, tn), lambda i, j: (i, ga_block * nj + j)),
                  pl.BlockSpec((tm, tn), lambda i, j: (i, gb_block * nj + j))],
        out_specs=pl.BlockSpec((tm, tn), lambda i, j: (i, j)),
        out_shape=jax.ShapeDtypeStruct((S, D), BF16),
        compiler_params=_params("arbitrary", "arbitrary"),
        name="merge",
    )(ya, yb, wa, wb, u, u)


def _outproj_kernel(m_ref, w_ref, x_ref, o_ref):
    o_ref[...] = x_ref[...] + jnp.dot(m_ref[...], w_ref[...], preferred_element_type=F32)


def _outproj(m, w, x, *, layer, tm, tn):
    S, D = x.shape
    return pl.pallas_call(
        _outproj_kernel,
        grid=(S // tm, D // tn),
        in_specs=[pl.BlockSpec((tm, D), lambda i, j: (i, 0)),
                  pl.BlockSpec((None, D, tn), lambda i, j: (layer, 0, j)),
                  pl.BlockSpec((tm, tn), lambda i, j: (i, j))],
        out_specs=pl.BlockSpec((tm, tn), lambda i, j: (i, j)),
        out_shape=jax.ShapeDtypeStruct((S, D), F32),
        compiler_params=_params("arbitrary", "arbitrary"),
        name="outproj",
    )(m, w, x)


def _final_norm_kernel(x_ref, w_ref, o_ref):
    x = x_ref[...]
    o_ref[...] = (x * lax.rsqrt(jnp.mean(x * x, axis=-1, keepdims=True) + EPS)) * w_ref[...]


def _final_norm(x, w, *, tm):
    S, D = x.shape
    return pl.pallas_call(
        _final_norm_kernel,
        grid=(S // tm,),
        in_specs=[pl.BlockSpec((tm, D), lambda i: (i, 0)), pl.BlockSpec((1, D), lambda i: (0, 0))],
        out_specs=pl.BlockSpec((tm, D), lambda i: (i, 0)),
        out_shape=jax.ShapeDtypeStruct((S, D), F32),
        compiler_params=_params("arbitrary"),
        name="final_norm",
    )(x, w)


def _tile(n, pref):
    t = min(n, pref)
    assert n % t == 0, (n, t)
    return t


def kernel(x, norm_w, w_in, b_in, qk_conv_w, mlstm_norm_w, w_a, dw_w, dw_b, ln_w, ln_b, w_b, w_out, final_norm_w):
    B, S, D = x.shape
    depth = norm_w.shape[0]
    W = w_a.shape[1]
    n_in = w_in.shape[2]
    heads = (n_in - 10 * W) // 4
    dh = W // heads
    assert B == 1 and W == D and w_b.shape[1] == W and n_in == 10 * W + 4 * heads
    assert S % (CHUNK * CHUNKS_PER_STEP) == 0 and dh % LANES == 0 and 2 * heads <= LANES
    assert heads % HEADS_PER_STEP == 0
    g0 = 5 * W
    ng = 4 * heads

    tm = _tile(S, 1024)
    tn = _tile(W, 1024)
    cw = _tile(W, LANES)
    tr = _tile(S, 256)

    Q, K_, V, O, ZA, GLA, GLB, ZB, GA, GB = range(10)
    U_O, U_ZA, U_ZB, U_GA, U_GB = range(5)

    w_main, w_g = _wprep(jnp.swapaxes(w_in, 1, 2), gate_start=g0, gate_width=ng, tn=_tile(W, 1024))
    b_main = jnp.concatenate([b_in[:, :g0], b_in[:, g0 + ng:]], axis=1)[:, None, :]
    b_g = jnp.pad(b_in[:, g0:g0 + ng], [(0, 0), (0, LANES - ng)])[:, None, :]
    wa16, wb16, wo16 = w_a.astype(BF16), w_b.astype(BF16), w_out.astype(BF16)
    row3 = lambda a: a[:, None, :]
    zero_b = jnp.zeros((depth, 1, W), F32)
    ident = lambda a: a

    xs = x.reshape(S, D)
    for l in range(depth):
        u, v, gates, hn = _inproj(xs, row3(norm_w), w_main, b_main, w_g, b_g, layer=l, width=W,
                                  blocks=(V, O, ZA, ZB, GA, GB),
                                  kinds=("values", "sigmoid", "silu", "silu", "sigmoid", "sigmoid"), tm=tm, tn=tn)
        qc = _proj_conv(hn, w_main, b_main, qk_conv_w, zero_b, layer=l, width=W, blocks=(Q,), w_off=0, post=_silu,
                        scale=1.0, transpose_out=False, out_dtype=BF16, tm=tm, tn=tn, cwid=cw, name="q_proj_conv")
        kt = _proj_conv(hn, w_main, b_main, qk_conv_w, zero_b, layer=l, width=W, blocks=(K_,), w_off=1, post=_silu,
                        scale=dh ** -0.5, transpose_out=True, out_dtype=F32, tm=tm, tn=tn, cwid=cw,
                        name="k_proj_conv")
        prep = _gateprep(gates, heads)
        hb = _mlstm(qc, kt, v, prep[5:], heads=heads, reverse=True)
        ya = _mlstm(qc, kt, v, prep[:5], heads=heads, reverse=False,
                    epilogue_args=(hb, u, U_O, U_ZA, row3(mlstm_norm_w), l))

        cconv = _proj_conv(hn, w_main, b_main, dw_w, row3(dw_b), layer=l, width=W, blocks=(GLA, GLB), w_off=0,
                           post=ident, scale=1.0, transpose_out=False, out_dtype=F32, tm=tm, tn=_tile(W, 512),
                           cwid=_tile(W, LANES), name="glu_proj_conv")
        yb = _post_b(cconv, u, row3(ln_w), row3(ln_b), layer=l, z_block=U_ZB, tm=tr)

        merged = _merge(ya, yb, wa16, wb16, u, layer=l, ga_block=U_GA, gb_block=U_GB, tm=tm, tn=_tile(D, 512))
        xs = _outproj(merged, wo16, xs, layer=l, tm=tm, tn=tn)

    return _final_norm(xs, final_norm_w[None, :], tm=tr).reshape(B, S, D)
```

```python
import functools

import jax
import jax.numpy as jnp
from jax import lax
from jax.experimental import pallas as pl
from jax.experimental.pallas import tpu as pltpu

F32 = jnp.float32
BF16 = jnp.bfloat16

EPS = 1e-6
LANES = 128
SUBLANES = 8
MXU_DIM = 256
CHUNK = MXU_DIM
VMEM_LIMIT_BYTES = 56 * 2**20
CONV_PAD = 16
CONV_ROWS = 128
CONV_PIECE_ROWS = 64
HEADS_PER_STEP = 2
CHUNKS_PER_STEP = 2


def _params(*sem):
    return pltpu.CompilerParams(dimension_semantics=sem, vmem_limit_bytes=VMEM_LIMIT_BYTES)


def _sigmoid(x):
    return 0.5 * jnp.tanh(0.5 * x) + 0.5


def _silu(x):
    return x * _sigmoid(x)


def _dot_t(a, bt):
    return lax.dot_general(a, bt, (((1,), (1,)), ((), ())), preferred_element_type=F32)


def _wprep_kernel(a_ref, nxt_ref, g_ref, o_ref, og_ref, *, first_shifted, shift):
    c = pl.program_id(1)

    @pl.when(c == 0)
    def _():
        og_ref[...] = jnp.zeros(og_ref.shape, BF16)
        og_ref[0:shift, :] = g_ref[...].astype(BF16)

    @pl.when(c < first_shifted)
    def _():
        o_ref[...] = a_ref[...].astype(BF16)

    @pl.when(c >= first_shifted)
    def _():
        rows = a_ref.shape[0]
        o_ref[0:rows - shift, :] = a_ref[shift:, :].astype(BF16)
        o_ref[rows - shift:, :] = nxt_ref[...].astype(BF16)


def _wprep(wt, *, width, before, after, gate_width, tn):
    depth, N, D = wt.shape
    gw = gate_width
    p = width // tn
    gate_start = (N - gw) // 2
    assert gate_start % width == 0 and tn % gw == 0 and gw % (2 * SUBLANES) == 0 and gw <= LANES
    assert list(before) == list(range(before[0], before[0] + len(before)))
    assert list(after) == list(range(after[0], after[0] + len(after)))
    nb = len(before) * p

    def src(c):
        return jnp.where(c < nb, before[0] * p + c, after[0] * p + c - nb)

    return pl.pallas_call(
        functools.partial(_wprep_kernel, first_shifted=nb, shift=gw),
        grid=(depth, nb + len(after) * p),
        in_specs=[pl.BlockSpec((None, tn, D), lambda l, c: (l, src(c), 0)),
                  pl.BlockSpec((None, gw, D), lambda l, c: (l, (src(c) + 1) * (tn // gw), 0)),
                  pl.BlockSpec((None, gw, D), lambda l, c: (l, gate_start // gw, 0))],
        out_specs=[pl.BlockSpec((None, tn, D), lambda l, c: (l, c, 0)),
                   pl.BlockSpec((None, LANES, D), lambda l, c: (l, 0, 0))],
        out_shape=[jax.ShapeDtypeStruct((depth, (len(before) + len(after)) * width, D), BF16),
                   jax.ShapeDtypeStruct((depth, LANES, D), BF16)],
        compiler_params=_params("arbitrary", "arbitrary"),
        name="wprep",
    )(wt, wt, wt)


def _inproj_kernel(x_ref, nw_ref, w_ref, b_ref, wg_ref, bg_ref, u_ref, v_ref, g_ref, hn_ref, *,
                   row_step, col_step, tiles_per_block, kinds):
    j = pl.program_id(1)
    blk = j // tiles_per_block

    @pl.when(j == 0)
    def _():
        for r in range(0, x_ref.shape[0], row_step):
            x = x_ref[r:r + row_step, :]
            ms = jnp.mean(x * x, axis=-1, keepdims=True)
            hn_ref[r:r + row_step, :] = ((x * lax.rsqrt(ms + EPS)) * nw_ref[...]).astype(BF16)
        g_ref[...] = _dot_t(hn_ref[...], wg_ref[...]) + bg_ref[...]

    def emit(out_ref, act):
        for c in range(0, w_ref.shape[0], col_step):
            y = _dot_t(hn_ref[...], w_ref[c:c + col_step, :]) + b_ref[:, c:c + col_step]
            out_ref[:, c:c + col_step] = act(y).astype(out_ref.dtype)

    any_of = lambda kind: functools.reduce(jnp.logical_or, [blk == k for k, kd in enumerate(kinds) if kd == kind])
    pl.when(any_of("values"))(lambda: emit(v_ref, lambda y: y))
    pl.when(any_of("sigmoid"))(lambda: emit(u_ref, _sigmoid))
    pl.when(any_of("silu"))(lambda: emit(u_ref, _silu))


def _inproj(x, nw, w, b, wg, bg, *, layer, width, blocks, kinds, tm, tn):
    S, D = x.shape
    NG = wg.shape[1]
    p = width // tn
    assert kinds[0] == "values" and "values" not in kinds[1:]

    def src(j):
        lb = j // p
        blk = functools.reduce(lambda acc, kb: jnp.where(lb == kb[0], kb[1], acc), list(enumerate(blocks)), 0)
        return blk * p + j % p

    return pl.pallas_call(
        functools.partial(_inproj_kernel, row_step=min(tm, 128), col_step=min(tn, MXU_DIM), tiles_per_block=p,
                          kinds=kinds),
        grid=(S // tm, len(blocks) * p),
        in_specs=[pl.BlockSpec((tm, D), lambda i, j: (i, 0)),
                  pl.BlockSpec((None, 1, D), lambda i, j: (layer, 0, 0)),
                  pl.BlockSpec((None, tn, D), lambda i, j: (layer, src(j), 0)),
                  pl.BlockSpec((None, 1, tn), lambda i, j: (layer, 0, src(j))),
                  pl.BlockSpec((None, NG, D), lambda i, j: (layer, 0, 0)),
                  pl.BlockSpec((None, 1, NG), lambda i, j: (layer, 0, 0))],
        out_specs=[pl.BlockSpec((tm, tn), lambda i, j: (i, jnp.maximum(j - p, 0))),
                   pl.BlockSpec((tm, tn), lambda i, j: (i, jnp.minimum(j, p - 1))),
                   pl.BlockSpec((tm, NG), lambda i, j: (i, 0)),
                   pl.BlockSpec((tm, D), lambda i, j: (i, 0))],
        out_shape=[jax.ShapeDtypeStruct((S, (len(blocks) - 1) * width), F32),
                   jax.ShapeDtypeStruct((S, width), BF16),
                   jax.ShapeDtypeStruct((S, NG), F32),
                   jax.ShapeDtypeStruct((S, D), BF16)],
        compiler_params=_params("arbitrary", "arbitrary"),
        name="inproj",
    )(x, nw, w, b, wg, bg)


def _proj_conv_kernel(*refs, n_mats, shifted, ktaps, post, scale, transpose_out, n_row_blocks, cwid):
    hn_ref = refs[0]
    w_refs = refs[1:1 + n_mats]
    nxt_refs = refs[1 + n_mats:1 + 2 * n_mats]
    b_refs = refs[1 + 2 * n_mats:1 + 3 * n_mats]
    cw_ref, cb_ref, o_ref, buf_ref, y_ref, wb_ref = refs[1 + 3 * n_mats:]
    i = pl.program_id(1)
    tm, tn = y_ref.shape
    P, R, RP = CONV_PAD, CONV_ROWS, CONV_PIECE_ROWS
    win_rows = RP + 2 * P
    half = ktaps // 2
    nrb = tm // R
    gw = nxt_refs[0].shape[0]

    def cast_weights():
        for m in range(n_mats):
            if shifted:
                wb_ref[m, 0:tn - gw, :] = w_refs[m][gw:, :].astype(BF16)
                wb_ref[m, tn - gw:, :] = nxt_refs[m][...].astype(BF16)
            else:
                wb_ref[m] = w_refs[m][...].astype(BF16)

    col_pieces = [slice(c, c + MXU_DIM) for c in range(0, tn, MXU_DIM)] if tn > MXU_DIM else [slice(0, tn)]

    def project(cs):
        ys = [_dot_t(hn_ref[...], wb_ref[m, cs, :]) + b_refs[m][:, cs] for m in range(n_mats)]
        y_ref[:, cs] = ys[0] if n_mats == 1 else ys[0] * _sigmoid(ys[1])

    def conv_piece(r0, cs):
        win = buf_ref[r0:r0 + win_rows, cs]
        acc = jnp.zeros((RP, cwid), F32) + cb_ref[:, cs]
        for sub in range(SUBLANES):
            taps = [d for d in range(ktaps) if (P + d - half) % SUBLANES == sub]
            if not taps:
                continue
            wsub = win if sub == 0 else pltpu.roll(win, win_rows - sub, 0)
            for d in taps:
                a = (P + d - half) // SUBLANES
                acc = acc + cw_ref[d:d + 1, cs] * wsub[SUBLANES * a:SUBLANES * a + RP, :]
        y = post(acc)
        return y if scale == 1.0 else y * scale

    def conv_rows(rb):
        for c0 in range(0, tn, cwid):
            cs = slice(c0, c0 + cwid)
            y = jnp.concatenate([conv_piece(rb * R + r, cs) for r in range(0, R, RP)], axis=0)
            if transpose_out:
                o_ref[cs, rb * R:(rb + 1) * R] = y.T.astype(o_ref.dtype)
            else:
                o_ref[rb * R:(rb + 1) * R, cs] = y.astype(o_ref.dtype)

    @pl.when(i == 0)
    def _():
        cast_weights()
        for cs in col_pieces:
            project(cs)
        buf_ref[0:P, :] = jnp.zeros((P, tn), F32)
        buf_ref[P:P + tm, :] = y_ref[...]

    @pl.when(jnp.logical_and(i > 0, i < n_row_blocks))
    def _():
        npc = len(col_pieces)
        for k, cs in enumerate(col_pieces):
            for rb in range(k * (nrb - 1) // npc, (k + 1) * (nrb - 1) // npc):
                conv_rows(rb)
            project(cs)
        buf_ref[P + tm:, :] = y_ref[0:P, :]
        conv_rows(nrb - 1)
        buf_ref[0:P, :] = buf_ref[tm:tm + P, :]
        buf_ref[P:P + tm, :] = y_ref[...]

    @pl.when(i == n_row_blocks)
    def _():
        buf_ref[P + tm:, :] = jnp.zeros((P, tn), F32)
        for rb in range(nrb):
            conv_rows(rb)


def _proj_conv(hn, wt, b, cw, cb, *, layer, width, blocks, gate_width, w_off, post, scale, transpose_out,
               out_dtype, tm, tn, cwid, name):
    S, D = hn.shape
    ktaps = cw.shape[1]
    gw = gate_width
    p = width // tn
    ni = S // tm
    n_mats = len(blocks)
    gate_block = (wt.shape[1] - gw) // (2 * width)
    shifted = blocks[0] >= gate_block
    assert all((blk >= gate_block) == shifted for blk in blocks) and tn % gw == 0
    row = lambda i: jnp.maximum(i - 1, 0)
    in_specs = [pl.BlockSpec((tm, D), lambda j, i: (jnp.minimum(i, ni - 1), 0))]
    in_specs += [pl.BlockSpec((None, tn, D), functools.partial(lambda j, i, o: (layer, o + j, 0), o=blk * p))
                 for blk in blocks]
    in_specs += [pl.BlockSpec((None, gw, D),
                              functools.partial(lambda j, i, o: (layer, (o + j + 1) * (tn // gw), 0), o=blk * p))
                 for blk in blocks]
    in_specs += [pl.BlockSpec((None, 1, tn), functools.partial(lambda j, i, o: (layer, 0, o + j), o=blk * p))
                 for blk in blocks]
    in_specs += [pl.BlockSpec((None, ktaps, tn), lambda j, i: (layer, 0, w_off * p + j)),
                 pl.BlockSpec((None, 1, tn), lambda j, i: (layer, 0, j))]
    if transpose_out:
        out_spec = pl.BlockSpec((tn, tm), lambda j, i: (j, row(i)))
        out_shape = jax.ShapeDtypeStruct((width, S), out_dtype)
    else:
        out_spec = pl.BlockSpec((tm, tn), lambda j, i: (row(i), j))
        out_shape = jax.ShapeDtypeStruct((S, width), out_dtype)
    return pl.pallas_call(
        functools.partial(_proj_conv_kernel, n_mats=n_mats, shifted=shifted, ktaps=ktaps, post=post, scale=scale,
                          transpose_out=transpose_out, n_row_blocks=ni, cwid=cwid),
        grid=(p, ni + 1),
        in_specs=in_specs,
        out_specs=out_spec,
        out_shape=out_shape,
        scratch_shapes=[pltpu.VMEM((tm + 2 * CONV_PAD, tn), F32), pltpu.VMEM((tm, tn), F32),
                        pltpu.VMEM((n_mats, tn, D), BF16)],
        compiler_params=_params("arbitrary", "arbitrary"),
        name=name,
    )(hn, *([wt] * (2 * n_mats)), *([b] * n_mats), cw, cb)


def _split3(x):
    a = x.astype(BF16)
    r = x - a.astype(F32)
    b = r.astype(BF16)
    c = (r - b.astype(F32)).astype(BF16)
    return a, b, c


def _gateprep_kernel(gf_ref, gb_ref, *out_and_scratch, heads):
    outs = out_and_scratch[:10]
    m_ref = out_and_scratch[10]
    L = CHUNK

    @pl.when(pl.program_id(0) == 0)
    def _():
        m_ref[...] = jnp.zeros(m_ref.shape, F32)

    row = lax.broadcasted_iota(jnp.int32, (L, L), 0)
    col = lax.broadcasted_iota(jnp.int32, (L, L), 1)
    for d, g_ref in enumerate((gf_ref, gb_ref)):
        wi_ref, wr_ref, er_ref, wk_ref, dc_ref = outs[5 * d:5 * d + 5]
        keep = (col <= row) if d == 0 else (col >= row)
        tri = keep.astype(BF16)
        t = g_ref[...]
        ls = jnp.minimum(t, 0.0) - jnp.log1p(jnp.exp(-jnp.abs(t)))
        g = sum(jnp.dot(tri, p, preferred_element_type=F32) for p in _split3(ls))
        gT = g.T
        tT = t.T
        for h in range(heads):
            c = heads * d + h
            ci = 2 * heads * d + h
            cf = ci + heads
            m = m_ref[c:c + 1, :]
            g_col = g[:, cf:cf + 1]
            g_row = gT[cf:cf + 1, :]
            i_row = tT[ci:ci + 1, :]
            a = g_col + m
            dmat = jnp.where(keep, g_col - g_row + i_row, -jnp.inf)
            dmax = jnp.max(dmat, axis=-1, keepdims=True)
            m_rows = jnp.maximum(a, dmax)
            wi_ref[h] = jnp.exp(dmat - jnp.maximum(a[:, 0:1], dmax))
            wr_ref[h] = jnp.exp(a - m_rows)
            er_ref[h] = jnp.exp(-m_rows)
            gtot = g_row[:, L - 1:L] if d == 0 else g_row[:, 0:1]
            b_row = gtot - g_row + i_row
            m_new = jnp.maximum(gtot + m, jnp.max(b_row, axis=-1, keepdims=True))
            wk_ref[h] = jnp.broadcast_to(jnp.exp(b_row - m_new[:, 0:1]), (SUBLANES, L))
            dc_ref[h] = jnp.broadcast_to(jnp.exp(gtot + m - m_new), (SUBLANES, LANES))
            m_ref[c:c + 1, :] = m_new


def _gateprep(gates, heads):
    S = gates.shape[0]
    L = CHUNK
    nc = S // L
    fw = lambda j: (0, j, 0)
    bw = lambda j: (0, nc - 1 - j, 0)
    out_specs, out_shape = [], []
    for idx in (fw, bw):
        out_specs += [pl.BlockSpec((heads, L, L), idx), pl.BlockSpec((heads, L, LANES), idx),
                      pl.BlockSpec((heads, L, LANES), idx), pl.BlockSpec((heads, SUBLANES, L), idx),
                      pl.BlockSpec((heads, SUBLANES, LANES), idx)]
        out_shape += [jax.ShapeDtypeStruct((heads, S, L), F32), jax.ShapeDtypeStruct((heads, S, LANES), F32),
                      jax.ShapeDtypeStruct((heads, S, LANES), F32),
                      jax.ShapeDtypeStruct((heads, nc * SUBLANES, L), F32),
                      jax.ShapeDtypeStruct((heads, nc * SUBLANES, LANES), F32)]
    return pl.pallas_call(
        functools.partial(_gateprep_kernel, heads=heads),
        grid=(nc,),
        in_specs=[pl.BlockSpec((L, LANES), lambda j: (j, 0)),
                  pl.BlockSpec((L, LANES), lambda j: (nc - 1 - j, 0))],
        out_specs=out_specs,
        out_shape=out_shape,
        scratch_shapes=[pltpu.VMEM((2 * heads, LANES), F32)],
        compiler_params=_params("arbitrary"),
        name="gateprep",
    )(gates, gates)


def _mlstm_kernel(*refs, reverse, epilogue, dh):
    q_ref, kt_ref, v_ref, wi_ref, wr_ref, er_ref, wk_ref, dc_ref = refs[:8]
    if epilogue:
        hb_ref, o_ref, z_ref, nw_ref = refs[8:12]
        out_ref, c_ref, cb_ref, n_ref = refs[12:]
    else:
        out_ref, c_ref, cb_ref, n_ref = refs[8:]
    L = CHUNK
    nlb = dh // LANES

    @pl.when(pl.program_id(1) == 0)
    def _():
        c_ref[...] = jnp.zeros(c_ref.shape, F32)
        cb_ref[...] = jnp.zeros(cb_ref.shape, BF16)
        n_ref[...] = jnp.zeros(n_ref.shape, F32)

    ones = jnp.ones((L, LANES), BF16)
    chunks = range(CHUNKS_PER_STEP)
    for t in (reversed(chunks) if reverse else chunks):
        rows = slice(t * L, (t + 1) * L)
        srow = slice(t * SUBLANES, t * SUBLANES + 1)
        for hh in range(HEADS_PER_STEP):
            cols = slice(hh * dh, (hh + 1) * dh)
            q = q_ref[rows, cols]
            kt = kt_ref[cols, rows]
            v1 = jnp.concatenate([v_ref[rows, cols], ones], axis=1)
            nvec = n_ref[hh]
            rhs = jnp.concatenate([kt.astype(BF16), nvec.astype(BF16)], axis=1)
            qk2 = jnp.dot(q, rhs, preferred_element_type=F32)
            s = qk2[:, :L] * wi_ref[hh, rows, :]
            w_inter = wr_ref[hh, rows, :]
            sv = jnp.dot(s.astype(BF16), v1, preferred_element_type=F32)
            den = sv[:, dh:] + w_inter * qk2[:, L:]
            rden = 1.0 / jnp.maximum(jnp.abs(den), er_ref[hh, rows, :])
            qc = jnp.dot(q, cb_ref[hh], preferred_element_type=F32)
            hblk = [(sv[:, b * LANES:(b + 1) * LANES] + w_inter * qc[:, b * LANES:(b + 1) * LANES]) * rden
                    for b in range(nlb)]
            if epilogue:
                ht = []
                for b in range(nlb):
                    cb = slice(hh * dh + b * LANES, hh * dh + (b + 1) * LANES)
                    ht.append(o_ref[rows, cb] * (hblk[b] + hb_ref[rows, cb]))
                ssq = sum(jnp.sum(x * x, axis=-1, keepdims=True) for x in ht)
                rinv = lax.rsqrt(ssq * (1.0 / dh) + EPS)
                for b in range(nlb):
                    cb = slice(hh * dh + b * LANES, hh * dh + (b + 1) * LANES)
                    y = ((ht[b] * rinv) * nw_ref[:, cb]) * z_ref[rows, cb]
                    out_ref[rows, cb] = y.astype(out_ref.dtype)
            else:
                for b in range(nlb):
                    cb = slice(hh * dh + b * LANES, hh * dh + (b + 1) * LANES)
                    out_ref[rows, cb] = hblk[b]
            wkt = kt * wk_ref[hh, srow, :]
            dec = dc_ref[hh, srow, :]
            dcm = jnp.dot(wkt.astype(BF16), v1, preferred_element_type=F32)
            n_ref[hh] = dec * nvec + dcm[:, dh:]
            cnew = jnp.concatenate([dec] * nlb, axis=1) * c_ref[hh] + dcm[:, :dh]
            c_ref[hh] = cnew
            cb_ref[hh] = cnew.astype(BF16)


def _mlstm(qc, kt, v, prep, *, heads, reverse, epilogue_args=None):
    S, W = qc.shape
    dh = W // heads
    L = CHUNK
    T, HP = CHUNKS_PER_STEP, HEADS_PER_STEP
    ng = S // (T * L)
    wi, wr, er, wk, dc = prep
    g = (lambda j: ng - 1 - j) if reverse else (lambda j: j)
    rowblk = pl.BlockSpec((T * L, HP * dh), lambda h, j: (g(j), h))
    in_specs = [rowblk,
                pl.BlockSpec((HP * dh, T * L), lambda h, j: (h, g(j))),
                rowblk,
                pl.BlockSpec((HP, T * L, L), lambda h, j: (h, g(j), 0)),
                pl.BlockSpec((HP, T * L, LANES), lambda h, j: (h, g(j), 0)),
                pl.BlockSpec((HP, T * L, LANES), lambda h, j: (h, g(j), 0)),
                pl.BlockSpec((HP, T * SUBLANES, L), lambda h, j: (h, g(j), 0)),
                pl.BlockSpec((HP, T * SUBLANES, LANES), lambda h, j: (h, g(j), 0))]
    args = [qc, kt, v, wi, wr, er, wk, dc]
    if epilogue_args is not None:
        hb, u, o_block, z_block, nw, layer = epilogue_args
        nhp = heads // HP
        in_specs += [rowblk,
                     pl.BlockSpec((T * L, HP * dh), lambda h, j: (g(j), o_block * nhp + h)),
                     pl.BlockSpec((T * L, HP * dh), lambda h, j: (g(j), z_block * nhp + h)),
                     pl.BlockSpec((None, 1, HP * dh), lambda h, j: (layer, 0, h))]
        args += [hb, u, u, nw]
        out_dtype = BF16
    else:
        out_dtype = F32
    return pl.pallas_call(
        functools.partial(_mlstm_kernel, reverse=reverse, epilogue=epilogue_args is not None, dh=dh),
        grid=(heads // HP, ng),
        in_specs=in_specs,
        out_specs=rowblk,
        out_shape=jax.ShapeDtypeStruct((S, W), out_dtype),
        scratch_shapes=[pltpu.VMEM((HP, dh, dh), F32), pltpu.VMEM((HP, dh, dh), BF16),
                        pltpu.VMEM((HP, dh, LANES), F32)],
        compiler_params=_params("arbitrary", "arbitrary"),
        name="mlstm_fw" if epilogue_args is not None else "mlstm_bw",
    )(*args)


def _post_b_kernel(c_ref, z_ref, lw_ref, lb_ref, y_ref):
    c = c_ref[...]
    mu = jnp.mean(c, axis=-1, keepdims=True)
    cc = c - mu
    var = jnp.mean(cc * cc, axis=-1, keepdims=True)
    y = cc * lax.rsqrt(var + EPS) * lw_ref[...] + lb_ref[...]
    y_ref[...] = (_silu(y) * z_ref[...]).astype(y_ref.dtype)


def _post_b(c, u, lw, lb, *, layer, z_block, tm):
    S, W = c.shape
    return pl.pallas_call(
        _post_b_kernel,
        grid=(S // tm,),
        in_specs=[pl.BlockSpec((tm, W), lambda i: (i, 0)),
                  pl.BlockSpec((tm, W), lambda i: (i, z_block)),
                  pl.BlockSpec((None, 1, W), lambda i: (layer, 0, 0)),
                  pl.BlockSpec((None, 1, W), lambda i: (layer, 0, 0))],
        out_specs=pl.BlockSpec((tm, W), lambda i: (i, 0)),
        out_shape=jax.ShapeDtypeStruct((S, W), BF16),
        compiler_params=_params("arbitrary"),
        name="post_b",
    )(c, u, lw, lb)


def _merge_kernel(ya_ref, yb_ref, wa_ref, wb_ref, ga_ref, gb_ref, m_ref):
    pa = jnp.dot(ya_ref[...], wa_ref[...], preferred_element_type=F32)
    pb = jnp.dot(yb_ref[...], wb_ref[...], preferred_element_type=F32)
    m_ref[...] = (ga_ref[...] * pa + gb_ref[...] * pb).astype(m_ref.dtype)


def _merge(ya, yb, wa, wb, u, *, layer, ga_block, gb_block, tm, tn):
    S, W = ya.shape
    D = wa.shape[2]
    nj = D // tn
    return pl.pallas_call(
        _merge_kernel,
        grid=(S // tm, nj),
        in_specs=[pl.BlockSpec((tm, W), lambda i, j: (i, 0)),
                  pl.BlockSpec((tm, W), lambda i, j: (i, 0)),
                  pl.BlockSpec((None, W, tn), lambda i, j: (layer, 0, j)),
                  pl.BlockSpec((None, W, tn), lambda i, j: (layer, 0, j)),
                  pl.BlockSpec((tm, tn), lambda i, j: (i, ga_block * nj + j)),
                  pl.BlockSpec((tm, tn), lambda i, j: (i, gb_block * nj + j))],
        out_specs=pl.BlockSpec((tm, tn), lambda i, j: (i, j)),
        out_shape=jax.ShapeDtypeStruct((S, D), BF16),
        compiler_params=_params("arbitrary", "arbitrary"),
        name="merge",
    )(ya, yb, wa, wb, u, u)


def _outproj_kernel(m_ref, w_ref, x_ref, o_ref):
    o_ref[...] = x_ref[...] + jnp.dot(m_ref[...], w_ref[...], preferred_element_type=F32)


def _outproj(m, w, x, *, layer, tm, tn):
    S, D = x.shape
    return pl.pallas_call(
        _outproj_kernel,
        grid=(S // tm, D // tn),
        in_specs=[pl.BlockSpec((tm, D), lambda i, j: (i, 0)),
                  pl.BlockSpec((None, D, tn), lambda i, j: (layer, 0, j)),
                  pl.BlockSpec((tm, tn), lambda i, j: (i, j))],
        out_specs=pl.BlockSpec((tm, tn), lambda i, j: (i, j)),
        out_shape=jax.ShapeDtypeStruct((S, D), F32),
        compiler_params=_params("arbitrary", "arbitrary"),
        name="outproj",
    )(m, w, x)


def _final_norm_kernel(x_ref, w_ref, o_ref):
    x = x_ref[...]
    o_ref[...] = (x * lax.rsqrt(jnp.mean(x * x, axis=-1, keepdims=True) + EPS)) * w_ref[...]


def _final_norm(x, w, *, tm):
    S, D = x.shape
    return pl.pallas_call(
        _final_norm_kernel,
        grid=(S // tm,),
        in_specs=[pl.BlockSpec((tm, D), lambda i: (i, 0)), pl.BlockSpec((1, D), lambda i: (0, 0))],
        out_specs=pl.BlockSpec((tm, D), lambda i: (i, 0)),
        out_shape=jax.ShapeDtypeStruct((S, D), F32),
        compiler_params=_params("arbitrary"),
        name="final_norm",
    )(x, w)


def _tile(n, pref):
    t = min(n, pref)
    assert n % t == 0, (n, t)
    return t


def kernel(x, norm_w, w_in, b_in, qk_conv_w, mlstm_norm_w, w_a, dw_w, dw_b, ln_w, ln_b, w_b, w_out, final_norm_w):
    B, S, D = x.shape
    depth = norm_w.shape[0]
    W = w_a.shape[1]
    n_in = w_in.shape[2]
    heads = (n_in - 10 * W) // 4
    dh = W // heads
    assert B == 1 and W == D and w_b.shape[1] == W and n_in == 10 * W + 4 * heads
    assert S % (CHUNK * CHUNKS_PER_STEP) == 0 and dh % LANES == 0 and 2 * heads <= LANES
    assert heads % HEADS_PER_STEP == 0
    g0 = 5 * W
    ng = 4 * heads

    tm = _tile(S, 1024)
    tn = _tile(W, 1024)
    cw = _tile(W, LANES)
    tr = _tile(S, 256)

    Q, K_, V, O, ZA, GLA, GLB, ZB, GA, GB = range(10)
    U_O, U_ZA, U_ZB, U_GA, U_GB = range(5)

    w_t = jnp.swapaxes(w_in, 1, 2)
    w_gv, w_g = _wprep(w_t, width=W, before=(V, O, ZA), after=(ZB, GA, GB), gate_width=ng, tn=_tile(W, 1024))
    b_main = jnp.concatenate([b_in[:, :g0], b_in[:, g0 + ng:]], axis=1)[:, None, :]
    b_gv = jnp.concatenate([b_main[:, :, V * W:(ZA + 1) * W], b_main[:, :, ZB * W:]], axis=2)
    b_g = jnp.pad(b_in[:, g0:g0 + ng], [(0, 0), (0, LANES - ng)])[:, None, :]
    wa16, wb16, wo16 = w_a.astype(BF16), w_b.astype(BF16), w_out.astype(BF16)
    row3 = lambda a: a[:, None, :]
    zero_b = jnp.zeros((depth, 1, W), F32)
    ident = lambda a: a

    xs = x.reshape(S, D)
    for l in range(depth):
        u, v, gates, hn = _inproj(xs, row3(norm_w), w_gv, b_gv, w_g, b_g, layer=l, width=W,
                                  blocks=tuple(range(6)),
                                  kinds=("values", "sigmoid", "silu", "silu", "sigmoid", "sigmoid"), tm=tm, tn=tn)
        qc = _proj_conv(hn, w_t, b_main, qk_conv_w, zero_b, layer=l, width=W, blocks=(Q,), gate_width=ng, w_off=0,
                        post=_silu, scale=1.0, transpose_out=False, out_dtype=BF16, tm=tm, tn=tn, cwid=cw,
                        name="q_proj_conv")
        kt = _proj_conv(hn, w_t, b_main, qk_conv_w, zero_b, layer=l, width=W, blocks=(K_,), gate_width=ng, w_off=1,
                        post=_silu, scale=dh ** -0.5, transpose_out=True, out_dtype=F32, tm=tm, tn=tn, cwid=cw,
                        name="k_proj_conv")
        prep = _gateprep(gates, heads)
        hb = _mlstm(qc, kt, v, prep[5:], heads=heads, reverse=True)
        ya = _mlstm(qc, kt, v, prep[:5], heads=heads, reverse=False,
                    epilogue_args=(hb, u, U_O, U_ZA, row3(mlstm_norm_w), l))

        cconv = _proj_conv(hn, w_t, b_main, dw_w, row3(dw_b), layer=l, width=W, blocks=(GLA, GLB), gate_width=ng,
                           w_off=0, post=ident, scale=1.0, transpose_out=False, out_dtype=F32, tm=tm,
                           tn=_tile(W, 512), cwid=_tile(W, LANES), name="glu_proj_conv")
        yb = _post_b(cconv, u, row3(ln_w), row3(ln_b), layer=l, z_block=U_ZB, tm=tr)

        merged = _merge(ya, yb, wa16, wb16, u, layer=l, ga_block=U_GA, gb_block=U_GB, tm=tm, tn=_tile(D, 512))
        xs = _outproj(merged, wo16, xs, layer=l, tm=tm, tn=tn)

    return _final_norm(xs, final_norm_w[None, :], tm=tr).reshape(B, S, D)
```

```python
import functools

import jax
import jax.numpy as jnp
from jax import lax
from jax.experimental import pallas as pl
from jax.experimental.pallas import tpu as pltpu

F32 = jnp.float32
BF16 = jnp.bfloat16

EPS = 1e-6
LANES = 128
SUBLANES = 8
MXU_DIM = 256
CHUNK = MXU_DIM
VMEM_LIMIT_BYTES = 56 * 2**20
CONV_PAD = 16
CONV_ROWS = 128
CONV_PIECE_ROWS = 64
HEADS_PER_STEP = 2
CHUNKS_PER_STEP_BW = 4
CHUNKS_PER_STEP_FW = 2


def _params(*sem):
    return pltpu.CompilerParams(dimension_semantics=sem, vmem_limit_bytes=VMEM_LIMIT_BYTES)


def _sigmoid(x):
    return 0.5 * jnp.tanh(0.5 * x) + 0.5


def _silu(x):
    return x * _sigmoid(x)


def _dot_t(a, bt):
    return lax.dot_general(a, bt, (((1,), (1,)), ((), ())), preferred_element_type=F32)


def _wprep_kernel(a_ref, nxt_ref, g_ref, o_ref, og_ref, *, first_shifted, shift):
    c = pl.program_id(1)

    @pl.when(c == 0)
    def _():
        og_ref[...] = jnp.zeros(og_ref.shape, BF16)
        og_ref[0:shift, :] = g_ref[...].astype(BF16)

    @pl.when(c < first_shifted)
    def _():
        o_ref[...] = a_ref[...].astype(BF16)

    @pl.when(c >= first_shifted)
    def _():
        rows = a_ref.shape[0]
        o_ref[0:rows - shift, :] = a_ref[shift:, :].astype(BF16)
        o_ref[rows - shift:, :] = nxt_ref[...].astype(BF16)


def _wprep(wt, *, width, before, after, gate_width, tn):
    depth, N, D = wt.shape
    gw = gate_width
    p = width // tn
    gate_start = (N - gw) // 2
    assert gate_start % width == 0 and tn % gw == 0 and gw % (2 * SUBLANES) == 0 and gw <= LANES
    assert list(before) == list(range(before[0], before[0] + len(before)))
    assert list(after) == list(range(after[0], after[0] + len(after)))
    nb = len(before) * p

    def src(c):
        return jnp.where(c < nb, before[0] * p + c, after[0] * p + c - nb)

    return pl.pallas_call(
        functools.partial(_wprep_kernel, first_shifted=nb, shift=gw),
        grid=(depth, nb + len(after) * p),
        in_specs=[pl.BlockSpec((None, tn, D), lambda l, c: (l, src(c), 0)),
                  pl.BlockSpec((None, gw, D), lambda l, c: (l, (src(c) + 1) * (tn // gw), 0)),
                  pl.BlockSpec((None, gw, D), lambda l, c: (l, gate_start // gw, 0))],
        out_specs=[pl.BlockSpec((None, tn, D), lambda l, c: (l, c, 0)),
                   pl.BlockSpec((None, LANES, D), lambda l, c: (l, 0, 0))],
        out_shape=[jax.ShapeDtypeStruct((depth, (len(before) + len(after)) * width, D), BF16),
                   jax.ShapeDtypeStruct((depth, LANES, D), BF16)],
        compiler_params=_params("arbitrary", "arbitrary"),
        name="wprep",
    )(wt, wt, wt)


def _inproj_kernel(x_ref, nw_ref, w_ref, b_ref, wg_ref, bg_ref, u_ref, v_ref, g_ref, hn_ref, *,
                   row_step, col_step, tiles_per_block, kinds):
    j = pl.program_id(1)
    blk = j // tiles_per_block

    @pl.when(j == 0)
    def _():
        for r in range(0, x_ref.shape[0], row_step):
            x = x_ref[r:r + row_step, :]
            ms = jnp.mean(x * x, axis=-1, keepdims=True)
            hn_ref[r:r + row_step, :] = ((x * lax.rsqrt(ms + EPS)) * nw_ref[...]).astype(BF16)
        g_ref[...] = _dot_t(hn_ref[...], wg_ref[...]) + bg_ref[...]

    def emit(out_ref, act):
        for c in range(0, w_ref.shape[0], col_step):
            y = _dot_t(hn_ref[...], w_ref[c:c + col_step, :]) + b_ref[:, c:c + col_step]
            out_ref[:, c:c + col_step] = act(y).astype(out_ref.dtype)

    any_of = lambda kind: functools.reduce(jnp.logical_or, [blk == k for k, kd in enumerate(kinds) if kd == kind])
    pl.when(any_of("values"))(lambda: emit(v_ref, lambda y: y))
    pl.when(any_of("sigmoid"))(lambda: emit(u_ref, _sigmoid))
    pl.when(any_of("silu"))(lambda: emit(u_ref, _silu))


def _inproj(x, nw, w, b, wg, bg, *, layer, width, blocks, kinds, tm, tn):
    S, D = x.shape
    NG = wg.shape[1]
    p = width // tn
    assert kinds[0] == "values" and "values" not in kinds[1:]

    def src(j):
        lb = j // p
        blk = functools.reduce(lambda acc, kb: jnp.where(lb == kb[0], kb[1], acc), list(enumerate(blocks)), 0)
        return blk * p + j % p

    return pl.pallas_call(
        functools.partial(_inproj_kernel, row_step=min(tm, 128), col_step=min(tn, MXU_DIM), tiles_per_block=p,
                          kinds=kinds),
        grid=(S // tm, len(blocks) * p),
        in_specs=[pl.BlockSpec((tm, D), lambda i, j: (i, 0)),
                  pl.BlockSpec((None, 1, D), lambda i, j: (layer, 0, 0)),
                  pl.BlockSpec((None, tn, D), lambda i, j: (layer, src(j), 0)),
                  pl.BlockSpec((None, 1, tn), lambda i, j: (layer, 0, src(j))),
                  pl.BlockSpec((None, NG, D), lambda i, j: (layer, 0, 0)),
                  pl.BlockSpec((None, 1, NG), lambda i, j: (layer, 0, 0))],
        out_specs=[pl.BlockSpec((tm, tn), lambda i, j: (i, jnp.maximum(j - p, 0))),
                   pl.BlockSpec((tm, tn), lambda i, j: (i, jnp.minimum(j, p - 1))),
                   pl.BlockSpec((tm, NG), lambda i, j: (i, 0)),
                   pl.BlockSpec((tm, D), lambda i, j: (i, 0))],
        out_shape=[jax.ShapeDtypeStruct((S, (len(blocks) - 1) * width), F32),
                   jax.ShapeDtypeStruct((S, width), BF16),
                   jax.ShapeDtypeStruct((S, NG), F32),
                   jax.ShapeDtypeStruct((S, D), BF16)],
        compiler_params=_params("arbitrary", "arbitrary"),
        name="inproj",
    )(x, nw, w, b, wg, bg)


def _proj_conv_kernel(*refs, n_mats, shifted, ktaps, post, scale, transpose_out, n_row_blocks, cwid):
    hn_ref = refs[0]
    w_refs = refs[1:1 + n_mats]
    nxt_refs = refs[1 + n_mats:1 + 2 * n_mats]
    b_refs = refs[1 + 2 * n_mats:1 + 3 * n_mats]
    cw_ref, cb_ref, o_ref, buf_ref, y_ref, wb_ref = refs[1 + 3 * n_mats:]
    i = pl.program_id(1)
    tm, tn = y_ref.shape
    P, R, RP = CONV_PAD, CONV_ROWS, CONV_PIECE_ROWS
    win_rows = RP + 2 * P
    half = ktaps // 2
    nrb = tm // R
    gw = nxt_refs[0].shape[0]

    def cast_weights():
        for m in range(n_mats):
            if shifted:
                wb_ref[m, 0:tn - gw, :] = w_refs[m][gw:, :].astype(BF16)
                wb_ref[m, tn - gw:, :] = nxt_refs[m][...].astype(BF16)
            else:
                wb_ref[m] = w_refs[m][...].astype(BF16)

    col_pieces = [slice(c, c + MXU_DIM) for c in range(0, tn, MXU_DIM)] if tn > MXU_DIM else [slice(0, tn)]

    def project(cs):
        ys = [_dot_t(hn_ref[...], wb_ref[m, cs, :]) + b_refs[m][:, cs] for m in range(n_mats)]
        y_ref[:, cs] = ys[0] if n_mats == 1 else ys[0] * _sigmoid(ys[1])

    def conv_piece(r0, cs):
        win = buf_ref[r0:r0 + win_rows, cs]
        acc = jnp.zeros((RP, cwid), F32) + cb_ref[:, cs]
        for sub in range(SUBLANES):
            taps = [d for d in range(ktaps) if (P + d - half) % SUBLANES == sub]
            if not taps:
                continue
            wsub = win if sub == 0 else pltpu.roll(win, win_rows - sub, 0)
            for d in taps:
                a = (P + d - half) // SUBLANES
                acc = acc + cw_ref[d:d + 1, cs] * wsub[SUBLANES * a:SUBLANES * a + RP, :]
        y = post(acc)
        return y if scale == 1.0 else y * scale

    def conv_rows(rb):
        for c0 in range(0, tn, cwid):
            cs = slice(c0, c0 + cwid)
            y = jnp.concatenate([conv_piece(rb * R + r, cs) for r in range(0, R, RP)], axis=0)
            if transpose_out:
                o_ref[cs, rb * R:(rb + 1) * R] = y.T.astype(o_ref.dtype)
            else:
                o_ref[rb * R:(rb + 1) * R, cs] = y.astype(o_ref.dtype)

    @pl.when(i == 0)
    def _():
        cast_weights()
        for cs in col_pieces:
            project(cs)
        buf_ref[0:P, :] = jnp.zeros((P, tn), F32)
        buf_ref[P:P + tm, :] = y_ref[...]

    @pl.when(jnp.logical_and(i > 0, i < n_row_blocks))
    def _():
        npc = len(col_pieces)
        for k, cs in enumerate(col_pieces):
            for rb in range(k * (nrb - 1) // npc, (k + 1) * (nrb - 1) // npc):
                conv_rows(rb)
            project(cs)
        buf_ref[P + tm:, :] = y_ref[0:P, :]
        conv_rows(nrb - 1)
        buf_ref[0:P, :] = buf_ref[tm:tm + P, :]
        buf_ref[P:P + tm, :] = y_ref[...]

    @pl.when(i == n_row_blocks)
    def _():
        buf_ref[P + tm:, :] = jnp.zeros((P, tn), F32)
        for rb in range(nrb):
            conv_rows(rb)


def _proj_conv(hn, wt, b, cw, cb, *, layer, width, blocks, gate_width, w_off, post, scale, transpose_out,
               out_dtype, tm, tn, cwid, name):
    S, D = hn.shape
    ktaps = cw.shape[1]
    gw = gate_width
    p = width // tn
    ni = S // tm
    n_mats = len(blocks)
    gate_block = (wt.shape[1] - gw) // (2 * width)
    shifted = blocks[0] >= gate_block
    assert all((blk >= gate_block) == shifted for blk in blocks) and tn % gw == 0
    row = lambda i: jnp.maximum(i - 1, 0)
    in_specs = [pl.BlockSpec((tm, D), lambda j, i: (jnp.minimum(i, ni - 1), 0))]
    in_specs += [pl.BlockSpec((None, tn, D), functools.partial(lambda j, i, o: (layer, o + j, 0), o=blk * p))
                 for blk in blocks]
    in_specs += [pl.BlockSpec((None, gw, D),
                              functools.partial(lambda j, i, o: (layer, (o + j + 1) * (tn // gw), 0), o=blk * p))
                 for blk in blocks]
    in_specs += [pl.BlockSpec((None, 1, tn), functools.partial(lambda j, i, o: (layer, 0, o + j), o=blk * p))
                 for blk in blocks]
    in_specs += [pl.BlockSpec((None, ktaps, tn), lambda j, i: (layer, 0, w_off * p + j)),
                 pl.BlockSpec((None, 1, tn), lambda j, i: (layer, 0, j))]
    if transpose_out:
        out_spec = pl.BlockSpec((tn, tm), lambda j, i: (j, row(i)))
        out_shape = jax.ShapeDtypeStruct((width, S), out_dtype)
    else:
        out_spec = pl.BlockSpec((tm, tn), lambda j, i: (row(i), j))
        out_shape = jax.ShapeDtypeStruct((S, width), out_dtype)
    return pl.pallas_call(
        functools.partial(_proj_conv_kernel, n_mats=n_mats, shifted=shifted, ktaps=ktaps, post=post, scale=scale,
                          transpose_out=transpose_out, n_row_blocks=ni, cwid=cwid),
        grid=(p, ni + 1),
        in_specs=in_specs,
        out_specs=out_spec,
        out_shape=out_shape,
        scratch_shapes=[pltpu.VMEM((tm + 2 * CONV_PAD, tn), F32), pltpu.VMEM((tm, tn), F32),
                        pltpu.VMEM((n_mats, tn, D), BF16)],
        compiler_params=_params("arbitrary", "arbitrary"),
        name=name,
    )(hn, *([wt] * (2 * n_mats)), *([b] * n_mats), cw, cb)


def _split3(x):
    a = x.astype(BF16)
    r = x - a.astype(F32)
    b = r.astype(BF16)
    c = (r - b.astype(F32)).astype(BF16)
    return a, b, c


def _gateprep_kernel(gf_ref, gb_ref, *out_and_scratch, heads):
    outs = out_and_scratch[:10]
    m_ref = out_and_scratch[10]
    L = CHUNK

    @pl.when(pl.program_id(0) == 0)
    def _():
        m_ref[...] = jnp.zeros(m_ref.shape, F32)

    row = lax.broadcasted_iota(jnp.int32, (L, L), 0)
    col = lax.broadcasted_iota(jnp.int32, (L, L), 1)
    for d, g_ref in enumerate((gf_ref, gb_ref)):
        wi_ref, wr_ref, er_ref, wk_ref, dc_ref = outs[5 * d:5 * d + 5]
        keep = (col <= row) if d == 0 else (col >= row)
        tri = keep.astype(BF16)
        t = g_ref[...]
        ls = jnp.minimum(t, 0.0) - jnp.log1p(jnp.exp(-jnp.abs(t)))
        g = sum(jnp.dot(tri, p, preferred_element_type=F32) for p in _split3(ls))
        gT = g.T
        tT = t.T
        for h in range(heads):
            c = heads * d + h
            ci = 2 * heads * d + h
            cf = ci + heads
            m = m_ref[c:c + 1, :]
            g_col = g[:, cf:cf + 1]
            g_row = gT[cf:cf + 1, :]
            i_row = tT[ci:ci + 1, :]
            a = g_col + m
            dmat = jnp.where(keep, g_col - g_row + i_row, -jnp.inf)
            dmax = jnp.max(dmat, axis=-1, keepdims=True)
            m_rows = jnp.maximum(a, dmax)
            wi_ref[h] = jnp.exp(dmat - jnp.maximum(a[:, 0:1], dmax))
            wr_ref[h] = jnp.exp(a - m_rows)
            er_ref[h] = jnp.exp(-m_rows)
            gtot = g_row[:, L - 1:L] if d == 0 else g_row[:, 0:1]
            b_row = gtot - g_row + i_row
            m_new = jnp.maximum(gtot + m, jnp.max(b_row, axis=-1, keepdims=True))
            wk_ref[h] = jnp.broadcast_to(jnp.exp(b_row - m_new[:, 0:1]), (SUBLANES, L))
            dc_ref[h] = jnp.broadcast_to(jnp.exp(gtot + m - m_new), (SUBLANES, LANES))
            m_ref[c:c + 1, :] = m_new


def _gateprep(gates, heads):
    S = gates.shape[0]
    L = CHUNK
    nc = S // L
    fw = lambda j: (0, j, 0)
    bw = lambda j: (0, nc - 1 - j, 0)
    out_specs, out_shape = [], []
    for idx in (fw, bw):
        out_specs += [pl.BlockSpec((heads, L, L), idx), pl.BlockSpec((heads, L, LANES), idx),
                      pl.BlockSpec((heads, L, LANES), idx), pl.BlockSpec((heads, SUBLANES, L), idx),
                      pl.BlockSpec((heads, SUBLANES, LANES), idx)]
        out_shape += [jax.ShapeDtypeStruct((heads, S, L), F32), jax.ShapeDtypeStruct((heads, S, LANES), F32),
                      jax.ShapeDtypeStruct((heads, S, LANES), F32),
                      jax.ShapeDtypeStruct((heads, nc * SUBLANES, L), F32),
                      jax.ShapeDtypeStruct((heads, nc * SUBLANES, LANES), F32)]
    return pl.pallas_call(
        functools.partial(_gateprep_kernel, heads=heads),
        grid=(nc,),
        in_specs=[pl.BlockSpec((L, LANES), lambda j: (j, 0)),
                  pl.BlockSpec((L, LANES), lambda j: (nc - 1 - j, 0))],
        out_specs=out_specs,
        out_shape=out_shape,
        scratch_shapes=[pltpu.VMEM((2 * heads, LANES), F32)],
        compiler_params=_params("arbitrary"),
        name="gateprep",
    )(gates, gates)


def _mlstm_kernel(*refs, reverse, epilogue, dh):
    q_ref, kt_ref, v_ref, wi_ref, wr_ref, er_ref, wk_ref, dc_ref = refs[:8]
    if epilogue:
        hb_ref, o_ref, z_ref, nw_ref = refs[8:12]
        out_ref, c_ref, cb_ref, n_ref = refs[12:]
    else:
        out_ref, c_ref, cb_ref, n_ref = refs[8:]
    L = CHUNK
    nlb = dh // LANES

    @pl.when(pl.program_id(1) == 0)
    def _():
        c_ref[...] = jnp.zeros(c_ref.shape, F32)
        cb_ref[...] = jnp.zeros(cb_ref.shape, BF16)
        n_ref[...] = jnp.zeros(n_ref.shape, F32)

    ones = jnp.ones((L, LANES), BF16)
    chunks = range(q_ref.shape[0] // L)
    for t in (reversed(chunks) if reverse else chunks):
        rows = slice(t * L, (t + 1) * L)
        srow = slice(t * SUBLANES, t * SUBLANES + 1)
        for hh in range(HEADS_PER_STEP):
            cols = slice(hh * dh, (hh + 1) * dh)
            q = q_ref[rows, cols]
            kt = kt_ref[cols, rows]
            v1 = jnp.concatenate([v_ref[rows, cols], ones], axis=1)
            nvec = n_ref[hh]
            rhs = jnp.concatenate([kt.astype(BF16), nvec.astype(BF16)], axis=1)
            qk2 = jnp.dot(q, rhs, preferred_element_type=F32)
            s = qk2[:, :L] * wi_ref[hh, rows, :]
            w_inter = wr_ref[hh, rows, :]
            sv = jnp.dot(s.astype(BF16), v1, preferred_element_type=F32)
            den = sv[:, dh:] + w_inter * qk2[:, L:]
            rden = 1.0 / jnp.maximum(jnp.abs(den), er_ref[hh, rows, :])
            qc = jnp.dot(q, cb_ref[hh], preferred_element_type=F32)
            hblk = [(sv[:, b * LANES:(b + 1) * LANES] + w_inter * qc[:, b * LANES:(b + 1) * LANES]) * rden
                    for b in range(nlb)]
            if epilogue:
                ht = []
                for b in range(nlb):
                    cb = slice(hh * dh + b * LANES, hh * dh + (b + 1) * LANES)
                    ht.append(o_ref[rows, cb] * (hblk[b] + hb_ref[rows, cb]))
                ssq = sum(jnp.sum(x * x, axis=-1, keepdims=True) for x in ht)
                rinv = lax.rsqrt(ssq * (1.0 / dh) + EPS)
                for b in range(nlb):
                    cb = slice(hh * dh + b * LANES, hh * dh + (b + 1) * LANES)
                    y = ((ht[b] * rinv) * nw_ref[:, cb]) * z_ref[rows, cb]
                    out_ref[rows, cb] = y.astype(out_ref.dtype)
            else:
                for b in range(nlb):
                    cb = slice(hh * dh + b * LANES, hh * dh + (b + 1) * LANES)
                    out_ref[rows, cb] = hblk[b]
            wkt = kt * wk_ref[hh, srow, :]
            dec = dc_ref[hh, srow, :]
            dcm = jnp.dot(wkt.astype(BF16), v1, preferred_element_type=F32)
            n_ref[hh] = dec * nvec + dcm[:, dh:]
            cnew = jnp.concatenate([dec] * nlb, axis=1) * c_ref[hh] + dcm[:, :dh]
            c_ref[hh] = cnew
            cb_ref[hh] = cnew.astype(BF16)


def _mlstm(qc, kt, v, prep, *, heads, reverse, chunks, epilogue_args=None):
    S, W = qc.shape
    dh = W // heads
    L = CHUNK
    T, HP = chunks, HEADS_PER_STEP
    ng = S // (T * L)
    wi, wr, er, wk, dc = prep
    g = (lambda j: ng - 1 - j) if reverse else (lambda j: j)
    rowblk = pl.BlockSpec((T * L, HP * dh), lambda h, j: (g(j), h))
    in_specs = [rowblk,
                pl.BlockSpec((HP * dh, T * L), lambda h, j: (h, g(j))),
                rowblk,
                pl.BlockSpec((HP, T * L, L), lambda h, j: (h, g(j), 0)),
                pl.BlockSpec((HP, T * L, LANES), lambda h, j: (h, g(j), 0)),
                pl.BlockSpec((HP, T * L, LANES), lambda h, j: (h, g(j), 0)),
                pl.BlockSpec((HP, T * SUBLANES, L), lambda h, j: (h, g(j), 0)),
                pl.BlockSpec((HP, T * SUBLANES, LANES), lambda h, j: (h, g(j), 0))]
    args = [qc, kt, v, wi, wr, er, wk, dc]
    if epilogue_args is not None:
        hb, u, o_block, z_block, nw, layer = epilogue_args
        nhp = heads // HP
        in_specs += [rowblk,
                     pl.BlockSpec((T * L, HP * dh), lambda h, j: (g(j), o_block * nhp + h)),
                     pl.BlockSpec((T * L, HP * dh), lambda h, j: (g(j), z_block * nhp + h)),
                     pl.BlockSpec((None, 1, HP * dh), lambda h, j: (layer, 0, h))]
        args += [hb, u, u, nw]
        out_dtype = BF16
    else:
        out_dtype = F32
    return pl.pallas_call(
        functools.partial(_mlstm_kernel, reverse=reverse, epilogue=epilogue_args is not None, dh=dh),
        grid=(heads // HP, ng),
        in_specs=in_specs,
        out_specs=rowblk,
        out_shape=jax.ShapeDtypeStruct((S, W), out_dtype),
        scratch_shapes=[pltpu.VMEM((HP, dh, dh), F32), pltpu.VMEM((HP, dh, dh), BF16),
                        pltpu.VMEM((HP, dh, LANES), F32)],
        compiler_params=_params("arbitrary", "arbitrary"),
        name="mlstm_fw" if epilogue_args is not None else "mlstm_bw",
    )(*args)


def _post_b_kernel(c_ref, z_ref, lw_ref, lb_ref, y_ref):
    c = c_ref[...]
    mu = jnp.mean(c, axis=-1, keepdims=True)
    cc = c - mu
    var = jnp.mean(cc * cc, axis=-1, keepdims=True)
    y = cc * lax.rsqrt(var + EPS) * lw_ref[...] + lb_ref[...]
    y_ref[...] = (_silu(y) * z_ref[...]).astype(y_ref.dtype)


def _post_b(c, u, lw, lb, *, layer, z_block, tm):
    S, W = c.shape
    return pl.pallas_call(
        _post_b_kernel,
        grid=(S // tm,),
        in_specs=[pl.BlockSpec((tm, W), lambda i: (i, 0)),
                  pl.BlockSpec((tm, W), lambda i: (i, z_block)),
                  pl.BlockSpec((None, 1, W), lambda i: (layer, 0, 0)),
                  pl.BlockSpec((None, 1, W), lambda i: (layer, 0, 0))],
        out_specs=pl.BlockSpec((tm, W), lambda i: (i, 0)),
        out_shape=jax.ShapeDtypeStruct((S, W), BF16),
        compiler_params=_params("arbitrary"),
        name="post_b",
    )(c, u, lw, lb)


def _merge_kernel(ya_ref, yb_ref, wa_ref, wb_ref, ga_ref, gb_ref, m_ref):
    pa = jnp.dot(ya_ref[...], wa_ref[...], preferred_element_type=F32)
    pb = jnp.dot(yb_ref[...], wb_ref[...], preferred_element_type=F32)
    m_ref[...] = (ga_ref[...] * pa + gb_ref[...] * pb).astype(m_ref.dtype)


def _merge(ya, yb, wa, wb, u, *, layer, ga_block, gb_block, tm, tn):
    S, W = ya.shape
    D = wa.shape[2]
    nj = D // tn
    return pl.pallas_call(
        _merge_kernel,
        grid=(S // tm, nj),
        in_specs=[pl.BlockSpec((tm, W), lambda i, j: (i, 0)),
                  pl.BlockSpec((tm, W), lambda i, j: (i, 0)),
                  pl.BlockSpec((None, W, tn), lambda i, j: (layer, 0, j)),
                  pl.BlockSpec((None, W, tn), lambda i, j: (layer, 0, j)),
                  pl.BlockSpec((tm, tn), lambda i, j: (i, ga_block * nj + j)),
                  pl.BlockSpec((tm, tn), lambda i, j: (i, gb_block * nj + j))],
        out_specs=pl.BlockSpec((tm, tn), lambda i, j: (i, j)),
        out_shape=jax.ShapeDtypeStruct((S, D), BF16),
        compiler_params=_params("arbitrary", "arbitrary"),
        name="merge",
    )(ya, yb, wa, wb, u, u)


def _outproj_kernel(m_ref, w_ref, x_ref, o_ref):
    o_ref[...] = x_ref[...] + jnp.dot(m_ref[...], w_ref[...], preferred_element_type=F32)


def _outproj(m, w, x, *, layer, tm, tn):
    S, D = x.shape
    return pl.pallas_call(
        _outproj_kernel,
        grid=(S // tm, D // tn),
        in_specs=[pl.BlockSpec((tm, D), lambda i, j: (i, 0)),
                  pl.BlockSpec((None, D, tn), lambda i, j: (layer, 0, j)),
                  pl.BlockSpec((tm, tn), lambda i, j: (i, j))],
        out_specs=pl.BlockSpec((tm, tn), lambda i, j: (i, j)),
        out_shape=jax.ShapeDtypeStruct((S, D), F32),
        compiler_params=_params("arbitrary", "arbitrary"),
        name="outproj",
    )(m, w, x)


def _final_norm_kernel(x_ref, w_ref, o_ref):
    x = x_ref[...]
    o_ref[...] = (x * lax.rsqrt(jnp.mean(x * x, axis=-1, keepdims=True) + EPS)) * w_ref[...]


def _final_norm(x, w, *, tm):
    S, D = x.shape
    return pl.pallas_call(
        _final_norm_kernel,
        grid=(S // tm,),
        in_specs=[pl.BlockSpec((tm, D), lambda i: (i, 0)), pl.BlockSpec((1, D), lambda i: (0, 0))],
        out_specs=pl.BlockSpec((tm, D), lambda i: (i, 0)),
        out_shape=jax.ShapeDtypeStruct((S, D), F32),
        compiler_params=_params("arbitrary"),
        name="final_norm",
    )(x, w)


def _tile(n, pref):
    t = min(n, pref)
    assert n % t == 0, (n, t)
    return t


def kernel(x, norm_w, w_in, b_in, qk_conv_w, mlstm_norm_w, w_a, dw_w, dw_b, ln_w, ln_b, w_b, w_out, final_norm_w):
    B, S, D = x.shape
    depth = norm_w.shape[0]
    W = w_a.shape[1]
    n_in = w_in.shape[2]
    heads = (n_in - 10 * W) // 4
    dh = W // heads
    assert B == 1 and W == D and w_b.shape[1] == W and n_in == 10 * W + 4 * heads
    assert S % (CHUNK * max(CHUNKS_PER_STEP_BW, CHUNKS_PER_STEP_FW)) == 0 and dh % LANES == 0 and 2 * heads <= LANES
    assert heads % HEADS_PER_STEP == 0
    g0 = 5 * W
    ng = 4 * heads

    tm = _tile(S, 1024)
    tn = _tile(W, 1024)
    cw = _tile(W, LANES)
    tr = _tile(S, 512)

    Q, K_, V, O, ZA, GLA, GLB, ZB, GA, GB = range(10)
    U_O, U_ZA, U_ZB, U_GA, U_GB = range(5)

    w_t = jnp.swapaxes(w_in, 1, 2)
    w_gv, w_g = _wprep(w_t, width=W, before=(V, O, ZA), after=(ZB, GA, GB), gate_width=ng, tn=_tile(W, 1024))
    b_main = jnp.concatenate([b_in[:, :g0], b_in[:, g0 + ng:]], axis=1)[:, None, :]
    b_gv = jnp.concatenate([b_main[:, :, V * W:(ZA + 1) * W], b_main[:, :, ZB * W:]], axis=2)
    b_g = jnp.pad(b_in[:, g0:g0 + ng], [(0, 0), (0, LANES - ng)])[:, None, :]
    wa16, wb16, wo16 = w_a.astype(BF16), w_b.astype(BF16), w_out.astype(BF16)
    row3 = lambda a: a[:, None, :]
    zero_b = jnp.zeros((depth, 1, W), F32)
    ident = lambda a: a

    xs = x.reshape(S, D)
    for l in range(depth):
        u, v, gates, hn = _inproj(xs, row3(norm_w), w_gv, b_gv, w_g, b_g, layer=l, width=W,
                                  blocks=tuple(range(6)),
                                  kinds=("values", "sigmoid", "silu", "silu", "sigmoid", "sigmoid"), tm=tm, tn=tn)
        qc = _proj_conv(hn, w_t, b_main, qk_conv_w, zero_b, layer=l, width=W, blocks=(Q,), gate_width=ng, w_off=0,
                        post=_silu, scale=1.0, transpose_out=False, out_dtype=BF16, tm=tm, tn=tn, cwid=cw,
                        name="q_proj_conv")
        kt = _proj_conv(hn, w_t, b_main, qk_conv_w, zero_b, layer=l, width=W, blocks=(K_,), gate_width=ng, w_off=1,
                        post=_silu, scale=dh ** -0.5, transpose_out=True, out_dtype=F32, tm=tm, tn=tn, cwid=cw,
                        name="k_proj_conv")
        prep = _gateprep(gates, heads)
        hb = _mlstm(qc, kt, v, prep[5:], heads=heads, reverse=True, chunks=CHUNKS_PER_STEP_BW)
        ya = _mlstm(qc, kt, v, prep[:5], heads=heads, reverse=False, chunks=CHUNKS_PER_STEP_FW,
                    epilogue_args=(hb, u, U_O, U_ZA, row3(mlstm_norm_w), l))

        cconv = _proj_conv(hn, w_t, b_main, dw_w, row3(dw_b), layer=l, width=W, blocks=(GLA, GLB), gate_width=ng,
                           w_off=0, post=ident, scale=1.0, transpose_out=False, out_dtype=F32, tm=tm,
                           tn=_tile(W, 512), cwid=_tile(W, LANES), name="glu_proj_conv")
        yb = _post_b(cconv, u, row3(ln_w), row3(ln_b), layer=l, z_block=U_ZB, tm=tr)

        merged = _merge(ya, yb, wa16, wb16, u, layer=l, ga_block=U_GA, gb_block=U_GB, tm=tm, tn=_tile(D, 512))
        xs = _outproj(merged, wo16, xs, layer=l, tm=tm, tn=tn)

    return _final_norm(xs, final_norm_w[None, :], tm=tr).reshape(B, S, D)
```

```python
import functools

import jax
import jax.numpy as jnp
from jax import lax
from jax.experimental import pallas as pl
from jax.experimental.pallas import tpu as pltpu

F32 = jnp.float32
BF16 = jnp.bfloat16

EPS = 1e-6
LANES = 128
SUBLANES = 8
MXU_DIM = 256
CHUNK = MXU_DIM
VMEM_LIMIT_BYTES = 56 * 2**20
CONV_PAD = 16
CONV_ROWS = 128
CONV_PIECE_ROWS = 64
HEADS_PER_STEP = 2
CHUNKS_PER_STEP_BW = 4
CHUNKS_PER_STEP_FW = 2


def _params(*sem):
    return pltpu.CompilerParams(dimension_semantics=sem, vmem_limit_bytes=VMEM_LIMIT_BYTES)


def _sigmoid(x):
    return 0.5 * jnp.tanh(0.5 * x) + 0.5


def _silu(x):
    return x * _sigmoid(x)


def _dot_t(a, bt):
    return lax.dot_general(a, bt, (((1,), (1,)), ((), ())), preferred_element_type=F32)


def _wprep_kernel(a_ref, nxt_ref, g_ref, o_ref, og_ref, *, first_shifted, shift):
    c = pl.program_id(1)

    @pl.when(c == 0)
    def _():
        og_ref[...] = jnp.zeros(og_ref.shape, BF16)
        og_ref[0:shift, :] = g_ref[...].astype(BF16)

    @pl.when(c < first_shifted)
    def _():
        o_ref[...] = a_ref[...].astype(BF16)

    @pl.when(c >= first_shifted)
    def _():
        rows = a_ref.shape[0]
        o_ref[0:rows - shift, :] = a_ref[shift:, :].astype(BF16)
        o_ref[rows - shift:, :] = nxt_ref[...].astype(BF16)


def _wprep(wt, *, width, before, after, gate_width, tn):
    depth, N, D = wt.shape
    gw = gate_width
    p = width // tn
    gate_start = (N - gw) // 2
    assert gate_start % width == 0 and tn % gw == 0 and gw % (2 * SUBLANES) == 0 and gw <= LANES
    assert list(before) == list(range(before[0], before[0] + len(before)))
    assert list(after) == list(range(after[0], after[0] + len(after)))
    nb = len(before) * p

    def src(c):
        return jnp.where(c < nb, before[0] * p + c, after[0] * p + c - nb)

    return pl.pallas_call(
        functools.partial(_wprep_kernel, first_shifted=nb, shift=gw),
        grid=(depth, nb + len(after) * p),
        in_specs=[pl.BlockSpec((None, tn, D), lambda l, c: (l, src(c), 0)),
                  pl.BlockSpec((None, gw, D), lambda l, c: (l, (src(c) + 1) * (tn // gw), 0)),
                  pl.BlockSpec((None, gw, D), lambda l, c: (l, gate_start // gw, 0))],
        out_specs=[pl.BlockSpec((None, tn, D), lambda l, c: (l, c, 0)),
                   pl.BlockSpec((None, LANES, D), lambda l, c: (l, 0, 0))],
        out_shape=[jax.ShapeDtypeStruct((depth, (len(before) + len(after)) * width, D), BF16),
                   jax.ShapeDtypeStruct((depth, LANES, D), BF16)],
        compiler_params=_params("arbitrary", "arbitrary"),
        name="wprep",
    )(wt, wt, wt)


def _inproj_kernel(x_ref, nw_ref, w_ref, b_ref, wg_ref, bg_ref, u_ref, v_ref, g_ref, hn_ref, *,
                   row_step, col_step, tiles_per_block, kinds):
    j = pl.program_id(1)
    blk = j // tiles_per_block

    @pl.when(j == 0)
    def _():
        for r in range(0, x_ref.shape[0], row_step):
            x = x_ref[r:r + row_step, :]
            ms = jnp.mean(x * x, axis=-1, keepdims=True)
            hn_ref[r:r + row_step, :] = ((x * lax.rsqrt(ms + EPS)) * nw_ref[...]).astype(BF16)
        g_ref[...] = _dot_t(hn_ref[...], wg_ref[...]) + bg_ref[...]

    def emit(out_ref, act):
        for c in range(0, w_ref.shape[0], col_step):
            y = _dot_t(hn_ref[...], w_ref[c:c + col_step, :]) + b_ref[:, c:c + col_step]
            out_ref[:, c:c + col_step] = act(y).astype(out_ref.dtype)

    any_of = lambda kind: functools.reduce(jnp.logical_or, [blk == k for k, kd in enumerate(kinds) if kd == kind])
    pl.when(any_of("values"))(lambda: emit(v_ref, lambda y: y))
    pl.when(any_of("sigmoid"))(lambda: emit(u_ref, _sigmoid))
    pl.when(any_of("silu"))(lambda: emit(u_ref, _silu))


def _inproj(x, nw, w, b, wg, bg, *, layer, width, blocks, kinds, tm, tn):
    S, D = x.shape
    NG = wg.shape[1]
    p = width // tn
    assert kinds[0] == "values" and "values" not in kinds[1:]

    def src(j):
        lb = j // p
        blk = functools.reduce(lambda acc, kb: jnp.where(lb == kb[0], kb[1], acc), list(enumerate(blocks)), 0)
        return blk * p + j % p

    return pl.pallas_call(
        functools.partial(_inproj_kernel, row_step=min(tm, 128), col_step=min(tn, MXU_DIM), tiles_per_block=p,
                          kinds=kinds),
        grid=(S // tm, len(blocks) * p),
        in_specs=[pl.BlockSpec((tm, D), lambda i, j: (i, 0)),
                  pl.BlockSpec((None, 1, D), lambda i, j: (layer, 0, 0)),
                  pl.BlockSpec((None, tn, D), lambda i, j: (layer, src(j), 0)),
                  pl.BlockSpec((None, 1, tn), lambda i, j: (layer, 0, src(j))),
                  pl.BlockSpec((None, NG, D), lambda i, j: (layer, 0, 0)),
                  pl.BlockSpec((None, 1, NG), lambda i, j: (layer, 0, 0))],
        out_specs=[pl.BlockSpec((tm, tn), lambda i, j: (i, jnp.maximum(j - p, 0))),
                   pl.BlockSpec((tm, tn), lambda i, j: (i, jnp.minimum(j, p - 1))),
                   pl.BlockSpec((tm, NG), lambda i, j: (i, 0)),
                   pl.BlockSpec((tm, D), lambda i, j: (i, 0))],
        out_shape=[jax.ShapeDtypeStruct((S, (len(blocks) - 1) * width), F32),
                   jax.ShapeDtypeStruct((S, width), BF16),
                   jax.ShapeDtypeStruct((S, NG), F32),
                   jax.ShapeDtypeStruct((S, D), BF16)],
        compiler_params=_params("arbitrary", "arbitrary"),
        name="inproj",
    )(x, nw, w, b, wg, bg)


def _proj_conv_kernel(*refs, n_mats, shifted, ktaps, post, scale, transpose_out, n_row_blocks, cwid):
    hn_ref = refs[0]
    w_refs = refs[1:1 + n_mats]
    nxt_refs = refs[1 + n_mats:1 + 2 * n_mats]
    b_refs = refs[1 + 2 * n_mats:1 + 3 * n_mats]
    cw_ref, cb_ref, o_ref, buf_ref, y_ref, wb_ref = refs[1 + 3 * n_mats:]
    i = pl.program_id(1)
    tm, tn = y_ref.shape
    P, R, RP = CONV_PAD, CONV_ROWS, CONV_PIECE_ROWS
    win_rows = RP + 2 * P
    half = ktaps // 2
    nrb = tm // R
    gw = nxt_refs[0].shape[0]

    def cast_weights():
        for m in range(n_mats):
            if shifted:
                wb_ref[m, 0:tn - gw, :] = w_refs[m][gw:, :].astype(BF16)
                wb_ref[m, tn - gw:, :] = nxt_refs[m][...].astype(BF16)
            else:
                wb_ref[m] = w_refs[m][...].astype(BF16)

    col_pieces = [slice(c, c + MXU_DIM) for c in range(0, tn, MXU_DIM)] if tn > MXU_DIM else [slice(0, tn)]

    def project(cs):
        ys = [_dot_t(hn_ref[...], wb_ref[m, cs, :]) + b_refs[m][:, cs] for m in range(n_mats)]
        y_ref[:, cs] = ys[0] if n_mats == 1 else ys[0] * _sigmoid(ys[1])

    def conv_piece(r0, cs):
        win = buf_ref[r0:r0 + win_rows, cs]
        acc = jnp.zeros((RP, cwid), F32) + cb_ref[:, cs]
        for sub in range(SUBLANES):
            taps = [d for d in range(ktaps) if (P + d - half) % SUBLANES == sub]
            if not taps:
                continue
            wsub = win if sub == 0 else pltpu.roll(win, win_rows - sub, 0)
            for d in taps:
                a = (P + d - half) // SUBLANES
                acc = acc + cw_ref[d:d + 1, cs] * wsub[SUBLANES * a:SUBLANES * a + RP, :]
        y = post(acc)
        return y if scale == 1.0 else y * scale

    def conv_rows(rb):
        for c0 in range(0, tn, cwid):
            cs = slice(c0, c0 + cwid)
            y = jnp.concatenate([conv_piece(rb * R + r, cs) for r in range(0, R, RP)], axis=0)
            if transpose_out:
                o_ref[cs, rb * R:(rb + 1) * R] = y.T.astype(o_ref.dtype)
            else:
                o_ref[rb * R:(rb + 1) * R, cs] = y.astype(o_ref.dtype)

    @pl.when(i == 0)
    def _():
        cast_weights()
        for cs in col_pieces:
            project(cs)
        buf_ref[0:P, :] = jnp.zeros((P, tn), F32)
        buf_ref[P:P + tm, :] = y_ref[...]

    @pl.when(jnp.logical_and(i > 0, i < n_row_blocks))
    def _():
        npc = len(col_pieces)
        for k, cs in enumerate(col_pieces):
            for rb in range(k * (nrb - 1) // npc, (k + 1) * (nrb - 1) // npc):
                conv_rows(rb)
            project(cs)
        buf_ref[P + tm:, :] = y_ref[0:P, :]
        conv_rows(nrb - 1)
        buf_ref[0:P, :] = buf_ref[tm:tm + P, :]
        buf_ref[P:P + tm, :] = y_ref[...]

    @pl.when(i == n_row_blocks)
    def _():
        buf_ref[P + tm:, :] = jnp.zeros((P, tn), F32)
        for rb in range(nrb):
            conv_rows(rb)


def _proj_conv(hn, wt, b, cw, cb, *, layer, width, blocks, gate_width, w_off, post, scale, transpose_out,
               out_dtype, tm, tn, cwid, name):
    S, D = hn.shape
    ktaps = cw.shape[1]
    gw = gate_width
    p = width // tn
    ni = S // tm
    n_mats = len(blocks)
    gate_block = (wt.shape[1] - gw) // (2 * width)
    shifted = blocks[0] >= gate_block
    assert all((blk >= gate_block) == shifted for blk in blocks) and tn % gw == 0
    row = lambda i: jnp.maximum(i - 1, 0)
    in_specs = [pl.BlockSpec((tm, D), lambda j, i: (jnp.minimum(i, ni - 1), 0))]
    in_specs += [pl.BlockSpec((None, tn, D), functools.partial(lambda j, i, o: (layer, o + j, 0), o=blk * p))
                 for blk in blocks]
    in_specs += [pl.BlockSpec((None, gw, D),
                              functools.partial(lambda j, i, o: (layer, (o + j + 1) * (tn // gw), 0), o=blk * p))
                 for blk in blocks]
    in_specs += [pl.BlockSpec((None, 1, tn), functools.partial(lambda j, i, o: (layer, 0, o + j), o=blk * p))
                 for blk in blocks]
    in_specs += [pl.BlockSpec((None, ktaps, tn), lambda j, i: (layer, 0, w_off * p + j)),
                 pl.BlockSpec((None, 1, tn), lambda j, i: (layer, 0, j))]
    if transpose_out:
        out_spec = pl.BlockSpec((tn, tm), lambda j, i: (j, row(i)))
        out_shape = jax.ShapeDtypeStruct((width, S), out_dtype)
    else:
        out_spec = pl.BlockSpec((tm, tn), lambda j, i: (row(i), j))
        out_shape = jax.ShapeDtypeStruct((S, width), out_dtype)
    return pl.pallas_call(
        functools.partial(_proj_conv_kernel, n_mats=n_mats, shifted=shifted, ktaps=ktaps, post=post, scale=scale,
                          transpose_out=transpose_out, n_row_blocks=ni, cwid=cwid),
        grid=(p, ni + 1),
        in_specs=in_specs,
        out_specs=out_spec,
        out_shape=out_shape,
        scratch_shapes=[pltpu.VMEM((tm + 2 * CONV_PAD, tn), F32), pltpu.VMEM((tm, tn), F32),
                        pltpu.VMEM((n_mats, tn, D), BF16)],
        compiler_params=_params("arbitrary", "arbitrary"),
        name=name,
    )(hn, *([wt] * (2 * n_mats)), *([b] * n_mats), cw, cb)


def _split3(x):
    a = x.astype(BF16)
    r = x - a.astype(F32)
    b = r.astype(BF16)
    c = (r - b.astype(F32)).astype(BF16)
    return a, b, c


def _gateprep_kernel(gf_ref, gb_ref, *out_and_scratch, heads):
    outs = out_and_scratch[:10]
    m_ref = out_and_scratch[10]
    L = CHUNK

    @pl.when(pl.program_id(0) == 0)
    def _():
        m_ref[...] = jnp.zeros(m_ref.shape, F32)

    row = lax.broadcasted_iota(jnp.int32, (L, L), 0)
    col = lax.broadcasted_iota(jnp.int32, (L, L), 1)
    for d, g_ref in enumerate((gf_ref, gb_ref)):
        wi_ref, wr_ref, er_ref, wk_ref, dc_ref = outs[5 * d:5 * d + 5]
        keep = (col <= row) if d == 0 else (col >= row)
        tri = keep.astype(BF16)
        t = g_ref[...]
        ls = jnp.minimum(t, 0.0) - jnp.log1p(jnp.exp(-jnp.abs(t)))
        g = sum(jnp.dot(tri, p, preferred_element_type=F32) for p in _split3(ls))
        gT = g.T
        tT = t.T
        for h in range(heads):
            c = heads * d + h
            ci = 2 * heads * d + h
            cf = ci + heads
            m = m_ref[c:c + 1, :]
            g_col = g[:, cf:cf + 1]
            g_row = gT[cf:cf + 1, :]
            i_row = tT[ci:ci + 1, :]
            a = g_col + m
            dmat = jnp.where(keep, g_col - g_row + i_row, -jnp.inf)
            dmax = jnp.max(dmat, axis=-1, keepdims=True)
            m_rows = jnp.maximum(a, dmax)
            wi_ref[h] = jnp.exp(dmat - jnp.maximum(a[:, 0:1], dmax))
            wr_ref[h] = jnp.exp(a - m_rows)
            er_ref[h] = jnp.exp(-m_rows)
            gtot = g_row[:, L - 1:L] if d == 0 else g_row[:, 0:1]
            b_row = gtot - g_row + i_row
            m_new = jnp.maximum(gtot + m, jnp.max(b_row, axis=-1, keepdims=True))
            wk_ref[h] = jnp.broadcast_to(jnp.exp(b_row - m_new[:, 0:1]), (SUBLANES, L))
            dc_ref[h] = jnp.broadcast_to(jnp.exp(gtot + m - m_new), (SUBLANES, LANES))
            m_ref[c:c + 1, :] = m_new


def _gateprep(gates, heads):
    S = gates.shape[0]
    L = CHUNK
    nc = S // L
    fw = lambda j: (0, j, 0)
    bw = lambda j: (0, nc - 1 - j, 0)
    out_specs, out_shape = [], []
    for idx in (fw, bw):
        out_specs += [pl.BlockSpec((heads, L, L), idx), pl.BlockSpec((heads, L, LANES), idx),
                      pl.BlockSpec((heads, L, LANES), idx), pl.BlockSpec((heads, SUBLANES, L), idx),
                      pl.BlockSpec((heads, SUBLANES, LANES), idx)]
        out_shape += [jax.ShapeDtypeStruct((heads, S, L), F32), jax.ShapeDtypeStruct((heads, S, LANES), F32),
                      jax.ShapeDtypeStruct((heads, S, LANES), F32),
                      jax.ShapeDtypeStruct((heads, nc * SUBLANES, L), F32),
                      jax.ShapeDtypeStruct((heads, nc * SUBLANES, LANES), F32)]
    return pl.pallas_call(
        functools.partial(_gateprep_kernel, heads=heads),
        grid=(nc,),
        in_specs=[pl.BlockSpec((L, LANES), lambda j: (j, 0)),
                  pl.BlockSpec((L, LANES), lambda j: (nc - 1 - j, 0))],
        out_specs=out_specs,
        out_shape=out_shape,
        scratch_shapes=[pltpu.VMEM((2 * heads, LANES), F32)],
        compiler_params=_params("arbitrary"),
        name="gateprep",
    )(gates, gates)


def _mlstm_kernel(*refs, reverse, epilogue, dh):
    q_ref, kt_ref, v_ref, wi_ref, wr_ref, er_ref, wk_ref, dc_ref = refs[:8]
    if epilogue:
        hb_ref, o_ref, z_ref, nw_ref = refs[8:12]
        out_ref, c_ref, cb_ref, n_ref = refs[12:]
    else:
        out_ref, c_ref, cb_ref, n_ref = refs[8:]
    L = CHUNK
    nlb = dh // LANES

    @pl.when(pl.program_id(1) == 0)
    def _():
        c_ref[...] = jnp.zeros(c_ref.shape, F32)
        cb_ref[...] = jnp.zeros(cb_ref.shape, BF16)
        n_ref[...] = jnp.zeros(n_ref.shape, F32)

    ones = jnp.ones((L, LANES), BF16)
    chunks = range(q_ref.shape[0] // L)
    for t in (reversed(chunks) if reverse else chunks):
        rows = slice(t * L, (t + 1) * L)
        srow = slice(t * SUBLANES, t * SUBLANES + 1)
        for hh in range(HEADS_PER_STEP):
            cols = slice(hh * dh, (hh + 1) * dh)
            q = q_ref[rows, cols]
            kt = kt_ref[cols, rows]
            v1 = jnp.concatenate([v_ref[rows, cols], ones], axis=1)
            nvec = n_ref[hh]
            rhs = jnp.concatenate([kt.astype(BF16), nvec.astype(BF16)], axis=1)
            qk2 = jnp.dot(q, rhs, preferred_element_type=F32)
            s = qk2[:, :L] * wi_ref[hh, rows, :]
            w_inter = wr_ref[hh, rows, :]
            sv = jnp.dot(s.astype(BF16), v1, preferred_element_type=F32)
            den = sv[:, dh:] + w_inter * qk2[:, L:]
            rden = 1.0 / jnp.maximum(jnp.abs(den), er_ref[hh, rows, :])
            qc = jnp.dot(q, cb_ref[hh], preferred_element_type=F32)
            hblk = [(sv[:, b * LANES:(b + 1) * LANES] + w_inter * qc[:, b * LANES:(b + 1) * LANES]) * rden
                    for b in range(nlb)]
            if epilogue:
                ht = []
                for b in range(nlb):
                    cb = slice(hh * dh + b * LANES, hh * dh + (b + 1) * LANES)
                    ht.append(o_ref[rows, cb] * (hblk[b] + hb_ref[rows, cb]))
                ssq = sum(jnp.sum(x * x, axis=-1, keepdims=True) for x in ht)
                rinv = lax.rsqrt(ssq * (1.0 / dh) + EPS)
                for b in range(nlb):
                    cb = slice(hh * dh + b * LANES, hh * dh + (b + 1) * LANES)
                    y = ((ht[b] * rinv) * nw_ref[:, cb]) * z_ref[rows, cb]
                    out_ref[rows, cb] = y.astype(out_ref.dtype)
            else:
                for b in range(nlb):
                    cb = slice(hh * dh + b * LANES, hh * dh + (b + 1) * LANES)
                    out_ref[rows, cb] = hblk[b]
            wkt = kt * wk_ref[hh, srow, :]
            dec = dc_ref[hh, srow, :]
            dcm = jnp.dot(wkt.astype(BF16), v1, preferred_element_type=F32)
            n_ref[hh] = dec * nvec + dcm[:, dh:]
            cnew = jnp.concatenate([dec] * nlb, axis=1) * c_ref[hh] + dcm[:, :dh]
            c_ref[hh] = cnew
            cb_ref[hh] = cnew.astype(BF16)


def _mlstm(qc, kt, v, prep, *, heads, reverse, chunks, epilogue_args=None):
    S, W = qc.shape
    dh = W // heads
    L = CHUNK
    T, HP = chunks, HEADS_PER_STEP
    ng = S // (T * L)
    wi, wr, er, wk, dc = prep
    g = (lambda j: ng - 1 - j) if reverse else (lambda j: j)
    rowblk = pl.BlockSpec((T * L, HP * dh), lambda h, j: (g(j), h))
    in_specs = [rowblk,
                pl.BlockSpec((HP * dh, T * L), lambda h, j: (h, g(j))),
                rowblk,
                pl.BlockSpec((HP, T * L, L), lambda h, j: (h, g(j), 0)),
                pl.BlockSpec((HP, T * L, LANES), lambda h, j: (h, g(j), 0)),
                pl.BlockSpec((HP, T * L, LANES), lambda h, j: (h, g(j), 0)),
                pl.BlockSpec((HP, T * SUBLANES, L), lambda h, j: (h, g(j), 0)),
                pl.BlockSpec((HP, T * SUBLANES, LANES), lambda h, j: (h, g(j), 0))]
    args = [qc, kt, v, wi, wr, er, wk, dc]
    if epilogue_args is not None:
        hb, u, o_block, z_block, nw, layer = epilogue_args
        nhp = heads // HP
        in_specs += [rowblk,
                     pl.BlockSpec((T * L, HP * dh), lambda h, j: (g(j), o_block * nhp + h)),
                     pl.BlockSpec((T * L, HP * dh), lambda h, j: (g(j), z_block * nhp + h)),
                     pl.BlockSpec((None, 1, HP * dh), lambda h, j: (layer, 0, h))]
        args += [hb, u, u, nw]
        out_dtype = BF16
    else:
        out_dtype = F32
    return pl.pallas_call(
        functools.partial(_mlstm_kernel, reverse=reverse, epilogue=epilogue_args is not None, dh=dh),
        grid=(heads // HP, ng),
        in_specs=in_specs,
        out_specs=rowblk,
        out_shape=jax.ShapeDtypeStruct((S, W), out_dtype),
        scratch_shapes=[pltpu.VMEM((HP, dh, dh), F32), pltpu.VMEM((HP, dh, dh), BF16),
                        pltpu.VMEM((HP, dh, LANES), F32)],
        compiler_params=_params("arbitrary", "arbitrary"),
        name="mlstm_fw" if epilogue_args is not None else "mlstm_bw",
    )(*args)


def _post_b_kernel(c_ref, z_ref, lw_ref, lb_ref, y_ref):
    rb = 2 * SUBLANES

    def body(r, carry):
        rows = pl.ds(pl.multiple_of(r * rb, rb), rb)
        c = c_ref[rows, :]
        mu = jnp.mean(c, axis=-1, keepdims=True)
        cc = c - mu
        var = jnp.mean(cc * cc, axis=-1, keepdims=True)
        y = cc * lax.rsqrt(var + EPS) * lw_ref[...] + lb_ref[...]
        y_ref[rows, :] = (_silu(y) * z_ref[rows, :]).astype(y_ref.dtype)
        return carry

    lax.fori_loop(0, c_ref.shape[0] // rb, body, 0, unroll=8)


def _post_b(c, u, lw, lb, *, layer, z_block, tm):
    S, W = c.shape
    return pl.pallas_call(
        _post_b_kernel,
        grid=(S // tm,),
        in_specs=[pl.BlockSpec((tm, W), lambda i: (i, 0)),
                  pl.BlockSpec((tm, W), lambda i: (i, z_block)),
                  pl.BlockSpec((None, 1, W), lambda i: (layer, 0, 0)),
                  pl.BlockSpec((None, 1, W), lambda i: (layer, 0, 0))],
        out_specs=pl.BlockSpec((tm, W), lambda i: (i, 0)),
        out_shape=jax.ShapeDtypeStruct((S, W), BF16),
        compiler_params=_params("arbitrary"),
        name="post_b",
    )(c, u, lw, lb)


def _merge_kernel(ya_ref, yb_ref, wa_ref, wb_ref, ga_ref, gb_ref, m_ref):
    pa = jnp.dot(ya_ref[...], wa_ref[...], preferred_element_type=F32)
    pb = jnp.dot(yb_ref[...], wb_ref[...], preferred_element_type=F32)
    m_ref[...] = (ga_ref[...] * pa + gb_ref[...] * pb).astype(m_ref.dtype)


def _merge(ya, yb, wa, wb, u, *, layer, ga_block, gb_block, tm, tn):
    S, W = ya.shape
    D = wa.shape[2]
    nj = D // tn
    return pl.pallas_call(
        _merge_kernel,
        grid=(S // tm, nj),
        in_specs=[pl.BlockSpec((tm, W), lambda i, j: (i, 0)),
                  pl.BlockSpec((tm, W), lambda i, j: (i, 0)),
                  pl.BlockSpec((None, W, tn), lambda i, j: (layer, 0, j)),
                  pl.BlockSpec((None, W, tn), lambda i, j: (layer, 0, j)),
                  pl.BlockSpec((tm, tn), lambda i, j: (i, ga_block * nj + j)),
                  pl.BlockSpec((tm, tn), lambda i, j: (i, gb_block * nj + j))],
        out_specs=pl.BlockSpec((tm, tn), lambda i, j: (i, j)),
        out_shape=jax.ShapeDtypeStruct((S, D), BF16),
        compiler_params=_params("arbitrary", "arbitrary"),
        name="merge",
    )(ya, yb, wa, wb, u, u)


def _outproj_kernel(m_ref, w_ref, x_ref, o_ref):
    o_ref[...] = x_ref[...] + jnp.dot(m_ref[...], w_ref[...], preferred_element_type=F32)


def _outproj(m, w, x, *, layer, tm, tn):
    S, D = x.shape
    return pl.pallas_call(
        _outproj_kernel,
        grid=(S // tm, D // tn),
        in_specs=[pl.BlockSpec((tm, D), lambda i, j: (i, 0)),
                  pl.BlockSpec((None, D, tn), lambda i, j: (layer, 0, j)),
                  pl.BlockSpec((tm, tn), lambda i, j: (i, j))],
        out_specs=pl.BlockSpec((tm, tn), lambda i, j: (i, j)),
        out_shape=jax.ShapeDtypeStruct((S, D), F32),
        compiler_params=_params("arbitrary", "arbitrary"),
        name="outproj",
    )(m, w, x)


def _final_norm_kernel(x_ref, w_ref, o_ref):
    x = x_ref[...]
    o_ref[...] = (x * lax.rsqrt(jnp.mean(x * x, axis=-1, keepdims=True) + EPS)) * w_ref[...]


def _final_norm(x, w, *, tm):
    S, D = x.shape
    return pl.pallas_call(
        _final_norm_kernel,
        grid=(S // tm,),
        in_specs=[pl.BlockSpec((tm, D), lambda i: (i, 0)), pl.BlockSpec((1, D), lambda i: (0, 0))],
        out_specs=pl.BlockSpec((tm, D), lambda i: (i, 0)),
        out_shape=jax.ShapeDtypeStruct((S, D), F32),
        compiler_params=_params("arbitrary"),
        name="final_norm",
    )(x, w)


def _tile(n, pref):
    t = min(n, pref)
    assert n % t == 0, (n, t)
    return t


def kernel(x, norm_w, w_in, b_in, qk_conv_w, mlstm_norm_w, w_a, dw_w, dw_b, ln_w, ln_b, w_b, w_out, final_norm_w):
    B, S, D = x.shape
    depth = norm_w.shape[0]
    W = w_a.shape[1]
    n_in = w_in.shape[2]
    heads = (n_in - 10 * W) // 4
    dh = W // heads
    assert B == 1 and W == D and w_b.shape[1] == W and n_in == 10 * W + 4 * heads
    assert S % (CHUNK * max(CHUNKS_PER_STEP_BW, CHUNKS_PER_STEP_FW)) == 0 and dh % LANES == 0 and 2 * heads <= LANES
    assert heads % HEADS_PER_STEP == 0
    g0 = 5 * W
    ng = 4 * heads

    tm = _tile(S, 1024)
    tn = _tile(W, 1024)
    cw = _tile(W, LANES)
    tr = _tile(S, 512)

    Q, K_, V, O, ZA, GLA, GLB, ZB, GA, GB = range(10)
    U_O, U_ZA, U_ZB, U_GA, U_GB = range(5)

    w_t = jnp.swapaxes(w_in, 1, 2)
    w_gv, w_g = _wprep(w_t, width=W, before=(V, O, ZA), after=(ZB, GA, GB), gate_width=ng, tn=_tile(W, 1024))
    b_main = jnp.concatenate([b_in[:, :g0], b_in[:, g0 + ng:]], axis=1)[:, None, :]
    b_gv = jnp.concatenate([b_main[:, :, V * W:(ZA + 1) * W], b_main[:, :, ZB * W:]], axis=2)
    b_g = jnp.pad(b_in[:, g0:g0 + ng], [(0, 0), (0, LANES - ng)])[:, None, :]
    wa16, wb16, wo16 = w_a.astype(BF16), w_b.astype(BF16), w_out.astype(BF16)
    row3 = lambda a: a[:, None, :]
    zero_b = jnp.zeros((depth, 1, W), F32)
    ident = lambda a: a

    xs = x.reshape(S, D)
    for l in range(depth):
        u, v, gates, hn = _inproj(xs, row3(norm_w), w_gv, b_gv, w_g, b_g, layer=l, width=W,
                                  blocks=tuple(range(6)),
                                  kinds=("values", "sigmoid", "silu", "silu", "sigmoid", "sigmoid"), tm=tm, tn=tn)
        qc = _proj_conv(hn, w_t, b_main, qk_conv_w, zero_b, layer=l, width=W, blocks=(Q,), gate_width=ng, w_off=0,
                        post=_silu, scale=1.0, transpose_out=False, out_dtype=BF16, tm=tm, tn=tn, cwid=cw,
                        name="q_proj_conv")
        kt = _proj_conv(hn, w_t, b_main, qk_conv_w, zero_b, layer=l, width=W, blocks=(K_,), gate_width=ng, w_off=1,
                        post=_silu, scale=dh ** -0.5, transpose_out=True, out_dtype=F32, tm=tm, tn=tn, cwid=cw,
                        name="k_proj_conv")
        prep = _gateprep(gates, heads)
        hb = _mlstm(qc, kt, v, prep[5:], heads=heads, reverse=True, chunks=CHUNKS_PER_STEP_BW)
        ya = _mlstm(qc, kt, v, prep[:5], heads=heads, reverse=False, chunks=CHUNKS_PER_STEP_FW,
                    epilogue_args=(hb, u, U_O, U_ZA, row3(mlstm_norm_w), l))

        cconv = _proj_conv(hn, w_t, b_main, dw_w, row3(dw_b), layer=l, width=W, blocks=(GLA, GLB), gate_width=ng,
                           w_off=0, post=ident, scale=1.0, transpose_out=False, out_dtype=F32, tm=tm,
                           tn=_tile(W, 512), cwid=_tile(W, LANES), name="glu_proj_conv")
        yb = _post_b(cconv, u, row3(ln_w), row3(ln_b), layer=l, z_block=U_ZB, tm=tr)

        merged = _merge(ya, yb, wa16, wb16, u, layer=l, ga_block=U_GA, gb_block=U_GB, tm=tm, tn=_tile(D, 512))
        xs = _outproj(merged, wo16, xs, layer=l, tm=tm, tn=tn)

    return _final_norm(xs, final_norm_w[None, :], tm=tr).reshape(B, S, D)
```
